```python
import math
import jax, jax.numpy as jnp
from jax import lax
import numpy as np


D_MODEL = 1024
BATCH = 4
SEQ = 8192
DEPTH = 4

MIX_WIDTH = 2 * D_MODEL
CONV_WIDTH = MIX_WIDTH // 2
CONV_K = 3
ATTN_HEAD_DIM = 64
ATTN_HEADS = (MIX_WIDTH - CONV_WIDTH) // ATTN_HEAD_DIM
ATTN_WIDTH = ATTN_HEADS * ATTN_HEAD_DIM
DILATED_PATTERNS = ((128, 1), (512, 4), (2048, 16))
ATTN_BLOCK = 128
REL_BUCKETS = 32
REL_MAX_DISTANCE = 2048
HGRN_HEADS = 16
HGRN_KEY_DIM = 128
HGRN_VAL_DIM = MIX_WIDTH // HGRN_HEADS
HGRN_CHUNK = 64
EPS = 1e-6
N_EVEN = (DEPTH + 1) // 2
N_ODD = DEPTH // 2
EVEN_SPLITS = (CONV_WIDTH, CONV_WIDTH, CONV_WIDTH, ATTN_WIDTH, ATTN_WIDTH, ATTN_WIDTH, MIX_WIDTH)
ODD_SPLITS = (HGRN_HEADS * HGRN_KEY_DIM, HGRN_HEADS * HGRN_KEY_DIM, HGRN_HEADS * HGRN_VAL_DIM, MIX_WIDTH)
EVEN_IN = sum(EVEN_SPLITS)
ODD_IN = sum(ODD_SPLITS)

kernel_name = 'hybrid_conv_dilattn_hgrn2_trunk'


def split_cols(t, sizes):
    idx = np.cumsum(sizes)[:-1].tolist()
    return jnp.split(t, idx, axis=-1)


def rms_norm(x, gain):
    xf = x.astype(jnp.float32)
    return xf * lax.rsqrt(jnp.mean(xf * xf, axis=-1, keepdims=True) + EPS) * gain.astype(jnp.float32)


def t5_bucket(distance):
    max_exact = REL_BUCKETS // 2
    scaled = jnp.log(jnp.maximum(distance, max_exact).astype(jnp.float32) / max_exact) / math.log(REL_MAX_DISTANCE / max_exact)
    large = jnp.minimum(max_exact + (scaled * (REL_BUCKETS - max_exact)).astype(jnp.int32), REL_BUCKETS - 1)
    return jnp.where(distance < max_exact, distance, large)


def dilated_window_attention(q, k, v, rel_bias, window, dilation):
    b, h, s, dh = q.shape
    blk = ATTN_BLOCK
    n_back = window // dilation
    assert n_back <= blk
    l_pad = -(-s // (dilation * blk)) * blk
    s_pad = l_pad * dilation
    nb = l_pad // blk

    def to_blocks(t):
        t = jnp.pad(t, ((0, 0), (0, 0), (0, s_pad - s), (0, 0)))
        t = t.reshape(b, h, l_pad, dilation, dh).transpose(0, 1, 3, 2, 4)
        return t.reshape(b, h, dilation, nb, blk, dh)

    def with_prev(t):
        prev = jnp.pad(t[:, :, :, :-1], ((0, 0), (0, 0), (0, 0), (1, 0), (0, 0), (0, 0)))
        return jnp.concatenate([prev, t], axis=4)

    qb = to_blocks(q)
    kk = with_prev(to_blocks(k))
    vv = with_prev(to_blocks(v))
    qi = jnp.arange(blk)[:, None]
    kj = jnp.arange(2 * blk)[None, :]
    delta = blk + qi - kj
    band = (delta >= 0) & (delta <= n_back)
    bucket = t5_bucket(jnp.maximum(delta, 0) * dilation)
    bias = rel_bias.astype(jnp.float32)[bucket].transpose(2, 0, 1)
    not_before_start = (jnp.arange(nb)[:, None, None] > 0) | (kj[None] >= blk)
    mask = band[None] & not_before_start
    logits = jnp.einsum('bhrnqd,bhrnkd->bhrnqk', qb, kk) + bias[None, :, None, None]
    logits = jnp.where(mask, logits, -jnp.inf)
    m = jnp.max(logits, axis=-1, keepdims=True)
    p = jnp.exp(logits - m)
    den = jnp.sum(p, axis=-1, keepdims=True)
    o = jnp.einsum('bhrnqk,bhrnkd->bhrnqd', p, vv) / den
    lse = (m + jnp.log(den))[..., 0]

    def from_blocks(t, tail):
        t = t.reshape((b, h, dilation, l_pad) + tail)
        t = jnp.moveaxis(t, 2, 3).reshape((b, h, s_pad) + tail)
        return t[:, :, :s]

    return from_blocks(o, (dh,)), from_blocks(lse, ())


def conv_attention_mixer(h, w_in, conv_w, q_gain, k_gain, rel_bias, w_out):
    b, s, _ = h.shape
    gate_b, gate_c, xa, q, k, v, z = split_cols(h @ w_in.astype(h.dtype), EVEN_SPLITS)
    u = gate_c * xa
    conv = lax.conv_general_dilated(u, conv_w[:, None, :].astype(u.dtype), window_strides=(1,),
                                    padding=((CONV_K - 1, 0),), dimension_numbers=('NWC', 'WIO', 'NWC'),
                                    feature_group_count=CONV_WIDTH)
    y_conv = gate_b * conv
    def heads(t, gain):
        return rms_norm(t.reshape(b, s, ATTN_HEADS, ATTN_HEAD_DIM), gain).transpose(0, 2, 1, 3)
    qh = heads(q, q_gain) * (ATTN_HEAD_DIM ** -0.5)
    kh = heads(k, k_gain)
    vh = v.astype(jnp.float32).reshape(b, s, ATTN_HEADS, ATTN_HEAD_DIM).transpose(0, 2, 1, 3)
    outs, lses = [], []
    for window, dilation in DILATED_PATTERNS:
        o_p, lse_p = dilated_window_attention(qh, kh, vh, rel_bias, window, dilation)
        outs.append(o_p)
        lses.append(lse_p)
    weights = jax.nn.softmax(jnp.stack(lses), axis=0)
    o = jnp.sum(weights[..., None] * jnp.stack(outs), axis=0)
    y_attn = o.transpose(0, 2, 1, 3).reshape(b, s, ATTN_WIDTH)
    y = jnp.concatenate([y_conv, y_attn.astype(y_conv.dtype)], axis=-1) * jax.nn.silu(z)
    return y @ w_out.astype(y.dtype)


def hgrn2_chunk_scan(q, k, v, log_f):
    b, h, s, dk = q.shape
    dv = v.shape[-1]
    c = HGRN_CHUNK
    nc = s // c

    def chunks(t):
        return t.reshape(b, h, nc, c, t.shape[-1]).transpose(2, 0, 1, 3, 4)

    causal = jnp.tril(jnp.ones((c, c), dtype=bool))

    def step(state, inp):
        qc, kc, vc, gc = inp
        gcum = jnp.cumsum(gc, axis=2)
        o_inter = jnp.einsum('bhtk,bhkv->bhtv', qc * jnp.exp(gcum), state)
        diff = gcum[:, :, :, None, :] - gcum[:, :, None, :, :]
        decay = jnp.exp(jnp.where(causal[:, :, None], diff, -jnp.inf))
        scores = jnp.einsum('bhtk,bhsk,bhtsk->bhts', qc, kc, decay)
        o_intra = jnp.einsum('bhts,bhsv->bhtv', scores, vc)
        g_last = gcum[:, :, -1:, :]
        new_state = state * jnp.exp(g_last[:, :, 0, :, None]) + jnp.einsum('bhsk,bhsv->bhkv', kc * jnp.exp(g_last - gcum), vc)
        return new_state, o_inter + o_intra

    state0 = jnp.zeros((b, h, dk, dv), jnp.float32)
    _, o = lax.scan(step, state0, (chunks(q), chunks(k), chunks(v), chunks(log_f)))
    return o.transpose(1, 2, 0, 3, 4).reshape(b, h, s, dv)


def hgrn2_mixer(h, w_in, lower_bound, o_gain, w_out):
    b, s, _ = h.shape
    q, f_pre, i, z = split_cols(h @ w_in.astype(h.dtype), ODD_SPLITS)
    f_pre = f_pre.astype(jnp.float32)
    lb = lower_bound
    log_f = jnp.logaddexp(jnp.log(lb), jnp.log1p(-lb) + jax.nn.log_sigmoid(f_pre))
    k = (1.0 - lb) * jax.nn.sigmoid(-f_pre)

    def heads(t, dim):
        return t.astype(jnp.float32).reshape(b, s, HGRN_HEADS, dim).transpose(0, 2, 1, 3)

    o = hgrn2_chunk_scan(heads(jax.nn.silu(q), HGRN_KEY_DIM), heads(k, HGRN_KEY_DIM),
                         heads(i, HGRN_VAL_DIM), heads(log_f, HGRN_KEY_DIM))
    o = rms_norm(o.transpose(0, 2, 1, 3), o_gain.reshape(HGRN_HEADS, HGRN_VAL_DIM)).reshape(b, s, MIX_WIDTH)
    y = o * jax.nn.silu(z)
    return y @ w_out.astype(y.dtype)


def setup_inputs(seed: int = 0) -> dict:
    key = jax.random.key(seed)
    ks = jax.random.split(key, 13)
    f32 = jnp.float32
    nrm = jax.random.normal
    return {
        'x': nrm(ks[0], (BATCH, SEQ, D_MODEL), f32),
        'ln_even': 1.0 + 0.1 * nrm(ks[1], (N_EVEN, D_MODEL), f32),
        'w_in_even': nrm(ks[2], (N_EVEN, D_MODEL, EVEN_IN), f32) * D_MODEL ** -0.5,
        'conv_w': nrm(ks[3], (N_EVEN, CONV_K, CONV_WIDTH), f32) * CONV_K ** -0.5,
        'q_gain': 1.0 + 0.1 * nrm(ks[4], (N_EVEN, ATTN_HEAD_DIM), f32),
        'k_gain': 1.0 + 0.1 * nrm(ks[5], (N_EVEN, ATTN_HEAD_DIM), f32),
        'w_out_even': nrm(ks[6], (N_EVEN, MIX_WIDTH, D_MODEL), f32) * MIX_WIDTH ** -0.5,
        'rel_bias': 0.5 * nrm(ks[7], (REL_BUCKETS, ATTN_HEADS), f32),
        'ln_odd': 1.0 + 0.1 * nrm(ks[8], (N_ODD, D_MODEL), f32),
        'w_in_odd': nrm(ks[9], (N_ODD, D_MODEL, ODD_IN), f32) * D_MODEL ** -0.5,
        'lower_bounds': 0.5 * nrm(ks[10], (N_ODD, HGRN_HEADS * HGRN_KEY_DIM), f32),
        'o_gain': 1.0 + 0.1 * nrm(ks[11], (N_ODD, MIX_WIDTH), f32),
        'w_out_odd': nrm(ks[12], (N_ODD, MIX_WIDTH, D_MODEL), f32) * MIX_WIDTH ** -0.5,
    }


def reference(x, ln_even, w_in_even, conv_w, q_gain, k_gain, w_out_even, rel_bias,
              ln_odd, w_in_odd, lower_bounds, o_gain, w_out_odd):
    lbs = jnp.cumsum(jax.nn.softmax(lower_bounds.astype(jnp.float32), axis=0), axis=0)
    lbs = lbs - lbs[0:1]
    for layer in range(DEPTH):
        j = layer // 2
        if layer % 2 == 0:
            hn = rms_norm(x, ln_even[j])
            delta = conv_attention_mixer(hn, w_in_even[j], conv_w[j], q_gain[j], k_gain[j], rel_bias, w_out_even[j])
        else:
            hn = rms_norm(x, ln_odd[j])
            delta = hgrn2_mixer(hn, w_in_odd[j], lbs[j], o_gain[j], w_out_odd[j])
        x = x + delta.astype(x.dtype)
    return x
```

```python
import functools
import math

import jax
import jax.numpy as jnp
import numpy as np
from jax import lax
from jax.experimental import pallas as pl
from jax.experimental.pallas import tpu as pltpu

F32 = jnp.float32
BF16 = jnp.bfloat16

D_MODEL = 1024
MIX_WIDTH = 2 * D_MODEL
CONV_WIDTH = MIX_WIDTH // 2
ATTN_HEAD_DIM = 64
ATTN_HEADS = 16
ATTN_BLOCK = 128
DILATED_PATTERNS = ((128, 1), (512, 4), (2048, 16))
REL_BUCKETS = 32
REL_MAX_DISTANCE = 2048
HGRN_HEADS = 16
HGRN_DIM = 128
IN_WIDTH = 8192
EPS = 1e-6
MASKED = -1e30

V7X_LANES = 128
V7X_SUBLANES = 8
V7X_VMEM_LIMIT_BYTES = 56 * 1024 * 1024

PROJ_TM = 512
PROJ_TN = 1024
ATTN_ROWS = 512
HGRN_ROWS = 512
HGRN_CHUNK = 128
HGRN_MAX_HALF_DECAY = 75.0


def _cparams(semantics):
    return pltpu.CompilerParams(dimension_semantics=semantics, vmem_limit_bytes=V7X_VMEM_LIMIT_BYTES)


def _silu(t):
    return t * jax.nn.sigmoid(t)


def _inproj_body(x_ref, g_ref, w_ref, cs_ref, bd_ref, o_ref, hn_ref, *, norm_tiles):
    j = pl.program_id(1)

    @pl.when(j == 0)
    def _():
        x = x_ref[...]
        ms = jnp.mean(x * x, axis=-1, keepdims=True)
        hn_ref[...] = (x * lax.rsqrt(ms + EPS) * g_ref[...]).astype(BF16)

    acc = jnp.dot(hn_ref[...], w_ref[...], preferred_element_type=F32)
    if not norm_tiles:
        o_ref[...] = acc.astype(o_ref.dtype)
        return

    is_norm = functools.reduce(jnp.logical_or, [j == t for t in norm_tiles])

    @pl.when(is_norm)
    def _():
        slab = bd_ref.shape[0]
        for s in range(acc.shape[1] // slab):
            a = acc[:, s * slab:(s + 1) * slab]
            ms = jnp.dot((a * a).astype(BF16), bd_ref[...], preferred_element_type=F32)
            o_ref[:, s * slab:(s + 1) * slab] = (
                a * lax.rsqrt(ms + EPS) * cs_ref[:, s * slab:(s + 1) * slab]).astype(o_ref.dtype)

    @pl.when(jnp.logical_not(is_norm))
    def _():
        o_ref[...] = acc.astype(o_ref.dtype)


def _norm_inproj(x2d, gain, w_bf16, colscale, norm_tiles):
    t, d = x2d.shape
    n = w_bf16.shape[1]
    tm, tn = PROJ_TM, PROJ_TN
    slab = 2 * V7X_LANES
    head_of_lane = np.arange(slab) // ATTN_HEAD_DIM
    bd = jnp.asarray((head_of_lane[:, None] == head_of_lane[None, :]) / ATTN_HEAD_DIM, BF16)
    return pl.pallas_call(
        functools.partial(_inproj_body, norm_tiles=norm_tiles),
        grid=(t // tm, n // tn),
        in_specs=[
            pl.BlockSpec((tm, d), lambda i, j: (i, 0)),
            pl.BlockSpec((1, d), lambda i, j: (0, 0)),
            pl.BlockSpec((d, tn), lambda i, j: (0, j)),
            pl.BlockSpec((1, tn), lambda i, j: (0, j)),
            pl.BlockSpec((slab, slab), lambda i, j: (0, 0)),
        ],
        out_specs=pl.BlockSpec((tm, tn), lambda i, j: (i, j)),
        out_shape=jax.ShapeDtypeStruct((t, n), BF16),
        scratch_shapes=[pltpu.VMEM((tm, d), BF16)],
        compiler_params=_cparams(("parallel", "arbitrary")),
        name="norm_inproj",
    )(x2d, gain.reshape(1, d).astype(F32), w_bf16, colscale, bd)


def _attn_body(*refs, has_prev, rows):
    if has_prev:
        (q_ref, kc_ref, kp_ref, vc_ref, vp_ref, bias_ref, po_ref, pl_ref,
         o_ref, l_ref, kw_ref, vw_ref) = refs
    else:
        (q_ref, kc_ref, kp_ref, vc_ref, vp_ref, bias_ref,
         o_ref, l_ref, kw_ref, vw_ref) = refs
    blk = ATTN_BLOCK
    t = pl.program_id(2)
    kw_ref[0:blk, :] = kp_ref[0]
    kw_ref[blk:blk + rows, :] = kc_ref[0]
    vw_ref[0:blk, :] = vp_ref[0]
    vw_ref[blk:blk + rows, :] = vc_ref[0]

    lane = lax.broadcasted_iota(jnp.int32, (blk, V7X_LANES), 1)
    low = lane < ATTN_HEAD_DIM
    low_bf = jnp.where(low, 1.0, 0.0).astype(BF16)
    high_bf = jnp.where(low, 0.0, 1.0).astype(BF16)
    col = lax.broadcasted_iota(jnp.int32, (2 * blk, 2 * blk), 1)
    n_pairs = ATTN_HEADS // 2

    def unit(idx, carry):
        qb = idx // n_pairs
        hp = idx % n_pairs
        r0 = pl.multiple_of(qb * blk, blk)
        c0 = pl.multiple_of(hp * V7X_LANES, V7X_LANES)
        q2 = q_ref[0, pl.ds(r0, blk), pl.ds(c0, V7X_LANES)]
        kw = kw_ref[pl.ds(r0, 2 * blk), pl.ds(c0, V7X_LANES)]
        vw = vw_ref[pl.ds(r0, 2 * blk), pl.ds(c0, V7X_LANES)]
        qs = jnp.concatenate([q2 * low_bf, q2 * high_bf], axis=0)
        s = lax.dot_general(qs, kw, (((1,), (1,)), ((), ())), preferred_element_type=F32)
        bias = jnp.concatenate([bias_ref[2 * hp], bias_ref[2 * hp + 1]], axis=0)
        s = s + bias
        no_prev = jnp.logical_and(t == 0, qb == 0)
        s = jnp.where(jnp.logical_and(no_prev, col < blk), MASKED, s)
        m = jnp.max(s, axis=-1, keepdims=True)
        p = jnp.exp(s - m)
        den = jnp.sum(p, axis=-1, keepdims=True)
        pv = jnp.dot(p.astype(BF16), vw, preferred_element_type=F32)
        lse = m + jnp.log(den)
        o2 = jnp.where(low, pv[0:blk] / den[0:blk], pv[blk:2 * blk] / den[blk:2 * blk])
        lse2 = jnp.where(low, lse[0:blk], lse[blk:2 * blk])
        if has_prev:
            po = po_ref[0, pl.ds(r0, blk), pl.ds(c0, V7X_LANES)].astype(F32)
            plse = pl_ref[0, pl.ds(r0, blk), pl.ds(c0, V7X_LANES)]
            top = jnp.maximum(plse, lse2)
            wa = jnp.exp(plse - top)
            wb = jnp.exp(lse2 - top)
            tot = wa + wb
            o2 = (wa * po + wb * o2) / tot
            lse2 = top + jnp.log(tot)
        o_ref[0, pl.ds(r0, blk), pl.ds(c0, V7X_LANES)] = o2.astype(o_ref.dtype)
        l_ref[0, pl.ds(r0, blk), pl.ds(c0, V7X_LANES)] = lse2
        return carry

    lax.fori_loop(0, (rows // blk) * n_pairs, unit, 0)


def _attention_pattern(proj, bias, dilation, prev, batch, seq):
    length = seq // dilation
    rows = min(ATTN_ROWS, length)
    blk = ATTN_BLOCK
    width = ATTN_HEADS * ATTN_HEAD_DIM
    tiles_in = IN_WIDTH // width
    projv = proj.reshape(batch, length, dilation * IN_WIDTH)
    q_col, k_col, v_col = 3, 4, 5

    def cur(cblk):
        return pl.BlockSpec((1, rows, width), lambda b, r, t: (b, t, r * tiles_in + cblk))

    def prv(cblk):
        return pl.BlockSpec((1, blk, width),
                            lambda b, r, t: (b, jnp.maximum(t * (rows // blk) - 1, 0), r * tiles_in + cblk))

    out_blk = pl.BlockSpec((1, rows, width), lambda b, r, t: (b, t, r))
    in_specs = [cur(q_col), cur(k_col), prv(k_col), cur(v_col), prv(v_col),
                pl.BlockSpec((ATTN_HEADS, blk, 2 * blk), lambda b, r, t: (0, 0, 0))]
    args = [projv, projv, projv, projv, projv, bias]
    if prev is not None:
        in_specs += [out_blk, out_blk]
        args += [prev[0].reshape(batch, length, dilation * width),
                 prev[1].reshape(batch, length, dilation * width)]
    o, lse = pl.pallas_call(
        functools.partial(_attn_body, has_prev=prev is not None, rows=rows),
        grid=(batch, dilation, length // rows),
        in_specs=in_specs,
        out_specs=[out_blk, out_blk],
        out_shape=[jax.ShapeDtypeStruct((batch, length, dilation * width), BF16),
                   jax.ShapeDtypeStruct((batch, length, dilation * width), F32)],
        scratch_shapes=[pltpu.VMEM((blk + rows, width), BF16), pltpu.VMEM((blk + rows, width), BF16)],
        compiler_params=_cparams(("parallel", "parallel", "arbitrary")),
        name=f"dilated_attention_d{dilation}",
    )(*args)
    return o.reshape(batch * seq, width), lse.reshape(batch * seq, width)


def _t5_bucket(distance):
    max_exact = REL_BUCKETS // 2
    scaled = jnp.log(jnp.maximum(distance, max_exact).astype(F32) / max_exact) / math.log(
        REL_MAX_DISTANCE / max_exact)
    large = jnp.minimum(max_exact + (scaled * (REL_BUCKETS - max_exact)).astype(jnp.int32), REL_BUCKETS - 1)
    return jnp.where(distance < max_exact, distance, large)


def _band_bias(rel_bias, window, dilation):
    blk = ATTN_BLOCK
    n_back = window // dilation
    qi = jnp.arange(blk)[:, None]
    kj = jnp.arange(2 * blk)[None, :]
    delta = blk + qi - kj
    band = (delta >= 0) & (delta <= n_back)
    bucket = _t5_bucket(jnp.maximum(delta, 0) * dilation)
    bias = rel_bias.astype(F32)[bucket].transpose(2, 0, 1)
    return jnp.where(band[None], bias, MASKED)


def _outproj_even_body(gb_ref, gc_ref, xa_ref, z_ref, oa_ref, gch_ref, xah_ref, cw_ref, w_ref, x_ref,
                       o_ref, *, tiles_per_seq):
    i = pl.program_id(0)
    tm = gc_ref.shape[0]
    u = gc_ref[...].astype(F32) * xa_ref[...].astype(F32)
    halo = gch_ref[...].astype(F32) * xah_ref[...].astype(F32)
    halo = jnp.where((i % tiles_per_seq) == 0, 0.0, halo)
    row = lax.broadcasted_iota(jnp.int32, u.shape, 0)
    h6 = halo[V7X_SUBLANES - 2:V7X_SUBLANES - 1]
    h7 = halo[V7X_SUBLANES - 1:V7X_SUBLANES]
    u1 = jnp.where(row >= 1, pltpu.roll(u, 1, axis=0), h7)
    u2 = jnp.where(row >= 2, pltpu.roll(u, 2, axis=0), jnp.where(row == 0, h6, h7))
    cw = cw_ref[...]
    conv = cw[0:1] * u2 + cw[1:2] * u1 + cw[2:3] * u
    gate = _silu(z_ref[...].astype(F32))
    half = CONV_WIDTH
    ya = (gb_ref[...].astype(F32) * conv * gate[:, :half]).astype(BF16)
    yb = (oa_ref[...].astype(F32) * gate[:, half:]).astype(BF16)
    acc = jnp.dot(ya, w_ref[0:half, :], preferred_element_type=F32)
    acc += jnp.dot(yb, w_ref[half:, :], preferred_element_type=F32)
    o_ref[...] = x_ref[...] + acc


def _outproj_even(proj, o_attn, conv_w, w_bf16, x2d, seq):
    t, d = x2d.shape
    tm = PROJ_TM
    w = CONV_WIDTH
    halo_blocks = tm // V7X_SUBLANES

    def colblk(c, width=w):
        return pl.BlockSpec((tm, width), lambda i: (i, c))

    def halo(c):
        return pl.BlockSpec((V7X_SUBLANES, w), lambda i: (jnp.maximum(i * halo_blocks - 1, 0), c))

    return pl.pallas_call(
        functools.partial(_outproj_even_body, tiles_per_seq=seq // tm),
        grid=(t // tm,),
        in_specs=[colblk(0), colblk(1), colblk(2), colblk(3, MIX_WIDTH),
                  pl.BlockSpec((tm, w), lambda i: (i, 0)),
                  halo(1), halo(2),
                  pl.BlockSpec(conv_w.shape, lambda i: (0, 0)),
                  pl.BlockSpec(w_bf16.shape, lambda i: (0, 0)),
                  pl.BlockSpec((tm, d), lambda i: (i, 0))],
        out_specs=pl.BlockSpec((tm, d), lambda i: (i, 0)),
        out_shape=jax.ShapeDtypeStruct((t, d), F32),
        compiler_params=_cparams(("parallel",)),
        name="conv_gate_outproj",
    )(proj, proj, proj, proj, o_attn, proj, proj, conv_w.astype(F32), w_bf16, x2d)


def _cumsum_rows(g):
    sub = V7X_SUBLANES
    row = lax.broadcasted_iota(jnp.int32, (sub, g.shape[1]), 0)
    out = []
    run = None
    for i in range(g.shape[0] // sub):
        x = g[i * sub:(i + 1) * sub]
        for s in (1, 2, 4):
            x = x + jnp.where(row >= s, pltpu.roll(x, s, axis=0), 0.0)
        if run is not None:
            x = x + run
        run = x[sub - 1:sub]
        out.append(x)
    return jnp.concatenate(out, axis=0)


def _hgrn_gates(f_pre, log_lb, log1m_lb, one_m_lb):
    log_sig = jnp.minimum(f_pre, 0.0) - jnp.log1p(jnp.exp(-jnp.abs(f_pre)))
    b = log1m_lb + log_sig
    hi = jnp.maximum(log_lb, b)
    log_f = hi + jnp.log1p(jnp.exp(-jnp.abs(log_lb - b)))
    k = one_m_lb * jax.nn.sigmoid(-f_pre)
    return log_f, k


def _hgrn_body(q_ref, f_ref, i_ref, z_ref, llb_ref, l1m_ref, oml_ref, og_ref, y_ref,
               st_ref, st0_ref, qrows_ref, frows_ref, krows_ref, *, rows):
    c = HGRN_CHUNK
    t = pl.program_id(2)

    @pl.when(t == 0)
    def _():
        st_ref[...] = jnp.zeros_like(st_ref)

    st0_ref[...] = st_ref[...]
    log_lb, log1m_lb, one_m_lb, o_gain = llb_ref[...], l1m_ref[...], oml_ref[...], og_ref[...]
    ri = lax.broadcasted_iota(jnp.int32, (c, c), 0)
    ci = lax.broadcasted_iota(jnp.int32, (c, c), 1)
    causal = ri >= ci

    def finish(o, sl):
        ms = jnp.mean(o * o, axis=-1, keepdims=True)
        y = o * lax.rsqrt(ms + EPS) * o_gain * _silu(z_ref[0, sl, :].astype(F32))
        y_ref[0, sl, :] = y.astype(y_ref.dtype)

    worst = jnp.zeros((1, HGRN_DIM), F32)
    for n in range(rows // c):
        sl = slice(n * c, (n + 1) * c)
        q = _silu(q_ref[0, sl, :].astype(F32))
        log_f, k = _hgrn_gates(f_ref[0, sl, :].astype(F32), log_lb, log1m_lb, one_m_lb)
        v = i_ref[0, sl, :]
        gcum = _cumsum_rows(log_f)
        g_last = gcum[c - 1:c]
        rho = 0.5 * g_last
        worst = jnp.minimum(worst, rho)
        qt = (q * jnp.exp(gcum - rho)).astype(BF16)
        kh = k * jnp.exp(rho - gcum)
        e_rho = jnp.exp(rho)
        st = st_ref[...]
        rhs = jnp.concatenate([kh.astype(BF16), (st * e_rho).astype(BF16)], axis=0)
        so = lax.dot_general(qt, rhs, (((1,), (1,)), ((), ())), preferred_element_type=F32)
        scores = jnp.where(causal, so[:, :c], 0.0).astype(BF16)
        o = so[:, c:] + jnp.dot(scores, v, preferred_element_type=F32)
        kd = (kh * e_rho).astype(BF16)
        st_ref[...] = st * jnp.exp(g_last) + lax.dot_general(
            v, kd, (((0,), (0,)), ((), ())), preferred_element_type=F32)
        finish(o, sl)

    @pl.when(jnp.min(worst) < -HGRN_MAX_HALF_DECAY)
    def _():
        lane = lax.broadcasted_iota(jnp.int32, (HGRN_DIM, c), 1)
        for n in range(rows // c):
            sl = slice(n * c, (n + 1) * c)
            log_f, k = _hgrn_gates(f_ref[0, sl, :].astype(F32), log_lb, log1m_lb, one_m_lb)
            qrows_ref[...] = _silu(q_ref[0, sl, :].astype(F32))
            frows_ref[...] = jnp.exp(log_f)
            krows_ref[...] = k
            vt = jnp.transpose(i_ref[0, sl, :].astype(F32))

            def step(s, carry):
                st, ot = carry
                vcol = jnp.sum(jnp.where(lane == s, vt, 0.0), axis=-1, keepdims=True)
                st = st * frows_ref[pl.ds(s, 1), :] + vcol * krows_ref[pl.ds(s, 1), :]
                ocol = jnp.sum(st * qrows_ref[pl.ds(s, 1), :], axis=-1, keepdims=True)
                ot = jnp.where(lane == s, ocol, ot)
                return st, ot

            st, ot = lax.fori_loop(0, c, step, (st0_ref[...], jnp.zeros((HGRN_DIM, c), F32)))
            st0_ref[...] = st
            finish(jnp.transpose(ot), sl)
        st_ref[...] = st0_ref[...]


def _hgrn2(proj, lb, o_gain, batch, seq):
    rows = HGRN_ROWS
    hd = HGRN_DIM
    projv = proj.reshape(batch, seq, IN_WIDTH)
    per_kind = HGRN_HEADS

    def part(kind):
        return pl.BlockSpec((1, rows, hd), lambda b, h, t: (b, t, kind * per_kind + h))

    def per_head():
        return pl.BlockSpec((1, hd), lambda b, h, t: (0, h))

    lb = lb.reshape(1, -1).astype(F32)
    y = pl.pallas_call(
        functools.partial(_hgrn_body, rows=rows),
        grid=(batch, HGRN_HEADS, seq // rows),
        in_specs=[part(0), part(1), part(2), part(3), per_head(), per_head(), per_head(), per_head()],
        out_specs=pl.BlockSpec((1, rows, hd), lambda b, h, t: (b, t, h)),
        out_shape=jax.ShapeDtypeStruct((batch, seq, MIX_WIDTH), BF16),
        scratch_shapes=[pltpu.VMEM((hd, hd), F32), pltpu.VMEM((hd, hd), F32)]
        + [pltpu.VMEM((HGRN_CHUNK, hd), F32)] * 3,
        compiler_params=_cparams(("parallel", "parallel", "arbitrary")),
        name="hgrn2_recurrence",
    )(projv, projv, projv, projv, jnp.log(lb), jnp.log1p(-lb), 1.0 - lb, o_gain.reshape(1, -1).astype(F32))
    return y.reshape(batch * seq, MIX_WIDTH)


def _outproj_body(y_ref, w_ref, x_ref, o_ref):
    o_ref[...] = x_ref[...] + jnp.dot(y_ref[...], w_ref[...], preferred_element_type=F32)


def _outproj(y, w_bf16, x2d):
    t, d = x2d.shape
    tm = PROJ_TM
    return pl.pallas_call(
        _outproj_body,
        grid=(t // tm,),
        in_specs=[pl.BlockSpec((tm, y.shape[1]), lambda i: (i, 0)),
                  pl.BlockSpec(w_bf16.shape, lambda i: (0, 0)),
                  pl.BlockSpec((tm, d), lambda i: (i, 0))],
        out_specs=pl.BlockSpec((tm, d), lambda i: (i, 0)),
        out_shape=jax.ShapeDtypeStruct((t, d), F32),
        compiler_params=_cparams(("parallel",)),
        name="outproj_residual",
    )(y, w_bf16, x2d)


def kernel(x, ln_even, w_in_even, conv_w, q_gain, k_gain, w_out_even, rel_bias, ln_odd, w_in_odd,
           lower_bounds, o_gain, w_out_odd):
    batch, seq, d = x.shape
    assert d == D_MODEL and seq % (DILATED_PATTERNS[-1][1] * ATTN_BLOCK) == 0 and seq % PROJ_TM == 0
    depth = ln_even.shape[0] + ln_odd.shape[0]
    x2d = x.reshape(batch * seq, d).astype(F32)

    lbs = jnp.cumsum(jax.nn.softmax(lower_bounds.astype(F32), axis=0), axis=0)
    lbs = lbs - lbs[0:1]
    biases = [_band_bias(rel_bias, window, dilation) for window, dilation in DILATED_PATTERNS]
    ones = jnp.ones((1, IN_WIDTH), F32)

    for layer in range(depth):
        j = layer // 2
        if layer % 2 == 0:
            width = ATTN_HEADS * ATTN_HEAD_DIM
            colscale = ones.at[0, 3 * width:4 * width].set(
                jnp.tile(q_gain[j].astype(F32), ATTN_HEADS) * ATTN_HEAD_DIM ** -0.5)
            colscale = colscale.at[0, 4 * width:5 * width].set(jnp.tile(k_gain[j].astype(F32), ATTN_HEADS))
            proj = _norm_inproj(x2d, ln_even[j], w_in_even[j].astype(BF16), colscale, norm_tiles=(3, 4))
            prev = None
            for bias, (_, dilation) in zip(biases, DILATED_PATTERNS):
                prev = _attention_pattern(proj, bias, dilation, prev, batch, seq)
            x2d = _outproj_even(proj, prev[0], conv_w[j], w_out_even[j].astype(BF16), x2d, seq)
        else:
            proj = _norm_inproj(x2d, ln_odd[j], w_in_odd[j].astype(BF16), ones, norm_tiles=())
            y = _hgrn2(proj, lbs[j], o_gain[j], batch, seq)
            x2d = _outproj(y, w_out_odd[j].astype(BF16), x2d)
    return x2d.reshape(batch, seq, d).astype(x.dtype)
```

```python
import functools
import math

import jax
import jax.numpy as jnp
import numpy as np
from jax import lax
from jax.experimental import pallas as pl
from jax.experimental.pallas import tpu as pltpu

F32 = jnp.float32
BF16 = jnp.bfloat16

D_MODEL = 1024
MIX_WIDTH = 2 * D_MODEL
CONV_WIDTH = MIX_WIDTH // 2
ATTN_HEAD_DIM = 64
ATTN_HEADS = 16
ATTN_WIDTH = ATTN_HEADS * ATTN_HEAD_DIM
ATTN_BLOCK = 128
DILATED_PATTERNS = ((128, 1), (512, 4), (2048, 16))
DILATION_STEP = 4
REL_BUCKETS = 32
REL_MAX_DISTANCE = 2048
HGRN_HEADS = 16
HGRN_DIM = 128
IN_WIDTH = 8192
QKV_TILES = (3, 4, 5)
EPS = 1e-6
MASKED = -1e30
LOG2E = math.log2(math.e)

V7X_LANES = 128
V7X_SUBLANES = 8
V7X_VMEM_LIMIT_BYTES = 56 * 1024 * 1024

PROJ_TM = 512
PROJ_TN = 1024
ATTN_ROWS = 512
ATTN_UNITS_PER_ITER = 2
HGRN_ROWS = 512
HGRN_CHUNK = 128
HGRN_MAX_HALF_DECAY = 75.0

N_SLABS = ATTN_WIDTH // V7X_LANES


def _cparams(semantics):
    return pltpu.CompilerParams(dimension_semantics=semantics, vmem_limit_bytes=V7X_VMEM_LIMIT_BYTES)


def _silu(t):
    return t * jax.nn.sigmoid(t)


def _inproj_body(x_ref, g_ref, w_ref, cs_ref, bd_ref, *rest, even):
    if even:
        o_ref, o4_ref, o16_ref, hn_ref, slab_ref = rest
    else:
        o_ref, hn_ref = rest
    j = pl.program_id(1)

    @pl.when(j == 0)
    def _():
        x = x_ref[...]
        ms = jnp.mean(x * x, axis=-1, keepdims=True)
        hn_ref[...] = (x * lax.rsqrt(ms + EPS) * g_ref[...]).astype(BF16)

    acc = jnp.dot(hn_ref[...], w_ref[...], preferred_element_type=F32)
    if not even:
        o_ref[...] = acc.astype(o_ref.dtype)
        return

    tm = acc.shape[0]

    def emit_qkv(res):
        o_ref[...] = res.astype(o_ref.dtype)
        for s in range(N_SLABS):
            slab_ref[s] = res[:, s * V7X_LANES:(s + 1) * V7X_LANES]
        for dil, ref in ((DILATED_PATTERNS[1][1], o4_ref), (DILATED_PATTERNS[2][1], o16_ref)):
            for r in range(dil):
                for s in range(N_SLABS):
                    ref[r, :, s * V7X_LANES:(s + 1) * V7X_LANES] = slab_ref[
                        s, pl.ds(r, tm // dil, stride=dil), :].astype(ref.dtype)

    q_tile, k_tile, v_tile = QKV_TILES

    @pl.when(jnp.logical_or(j == q_tile, j == k_tile))
    def _():
        width = bd_ref.shape[0]
        parts = []
        for s in range(acc.shape[1] // width):
            a = acc[:, s * width:(s + 1) * width]
            ms = jnp.dot((a * a).astype(BF16), bd_ref[...], preferred_element_type=F32)
            parts.append(a * lax.rsqrt(ms + EPS) * cs_ref[:, s * width:(s + 1) * width])
        emit_qkv(jnp.concatenate(parts, axis=1))

    @pl.when(j == v_tile)
    def _():
        emit_qkv(acc)

    @pl.when(jnp.logical_or(j < q_tile, j > v_tile))
    def _():
        o_ref[...] = acc.astype(o_ref.dtype)


def _norm_inproj(x2d, gain, w_bf16, colscale, even, batch, seq):
    t, d = x2d.shape
    n = w_bf16.shape[1]
    tm, tn = PROJ_TM, PROJ_TN
    width = 2 * V7X_LANES
    head_of_lane = np.arange(width) // ATTN_HEAD_DIM
    bd = jnp.asarray((head_of_lane[:, None] == head_of_lane[None, :]) / ATTN_HEAD_DIM, BF16)
    tiles_per_seq = seq // tm
    out_specs = [pl.BlockSpec((tm, tn), lambda i, j: (i, j))]
    out_shape = [jax.ShapeDtypeStruct((t, n), BF16)]
    scratch = [pltpu.VMEM((tm, d), BF16)]
    if even:
        for _, dil in DILATED_PATTERNS[1:]:
            out_specs.append(pl.BlockSpec(
                (None, dil, tm // dil, tn),
                lambda i, j: (i // tiles_per_seq, 0, i % tiles_per_seq,
                              jnp.clip(j - QKV_TILES[0], 0, len(QKV_TILES) - 1))))
            out_shape.append(jax.ShapeDtypeStruct((batch, dil, seq // dil, len(QKV_TILES) * tn), BF16))
        scratch.append(pltpu.VMEM((N_SLABS, tm, V7X_LANES), F32))
    return pl.pallas_call(
        functools.partial(_inproj_body, even=even),
        grid=(t // tm, n // tn),
        in_specs=[
            pl.BlockSpec((tm, d), lambda i, j: (i, 0)),
            pl.BlockSpec((1, d), lambda i, j: (0, 0)),
            pl.BlockSpec((d, tn), lambda i, j: (0, j)),
            pl.BlockSpec((1, tn), lambda i, j: (0, j)),
            pl.BlockSpec((width, width), lambda i, j: (0, 0)),
        ],
        out_specs=out_specs,
        out_shape=out_shape,
        scratch_shapes=scratch,
        compiler_params=_cparams(("parallel", "arbitrary")),
        name="norm_inproj_even" if even else "norm_inproj_odd",
    )(x2d, gain.reshape(1, d).astype(F32), w_bf16, colscale, bd)


def _attn_body(*refs, has_prev, scatter, rows):
    refs = list(refs)
    q_ref, kc_ref, kp_ref, vc_ref, vp_ref, bias_ref = refs[:6]
    del refs[:6]
    if has_prev:
        po_ref, pl_ref = refs[:2]
        del refs[:2]
    if scatter:
        o_ref, l_ref, kw_ref, vw_ref, so_ref, sl_ref = refs
    else:
        o_ref, kw_ref, vw_ref = refs
    blk = ATTN_BLOCK
    t = pl.program_id(2)
    kw_ref[0:blk, :] = kp_ref[...]
    kw_ref[blk:blk + rows, :] = kc_ref[...]
    vw_ref[0:blk, :] = vp_ref[...]
    vw_ref[blk:blk + rows, :] = vc_ref[...]

    lane = lax.broadcasted_iota(jnp.int32, (blk, V7X_LANES), 1)
    low = lane < ATTN_HEAD_DIM
    low_bf = jnp.where(low, 1.0, 0.0).astype(BF16)
    high_bf = jnp.where(low, 0.0, 1.0).astype(BF16)
    n_pairs = ATTN_HEADS // 2
    step = DILATION_STEP
    sub = blk // step

    def unit(idx, slot):
        qb = idx // n_pairs
        hp = idx % n_pairs
        r0 = pl.multiple_of(qb * blk, blk)
        c0 = pl.multiple_of(hp * V7X_LANES, V7X_LANES)
        q2 = q_ref[pl.ds(r0, blk), pl.ds(c0, V7X_LANES)]
        kw = kw_ref[pl.ds(r0, 2 * blk), pl.ds(c0, V7X_LANES)]
        vw = vw_ref[pl.ds(r0, 2 * blk), pl.ds(c0, V7X_LANES)]
        qs = jnp.concatenate([q2 * low_bf, q2 * high_bf], axis=0)
        s = lax.dot_general(qs, kw, (((1,), (1,)), ((), ())), preferred_element_type=F32)
        first = jnp.logical_and(t == 0, qb == 0).astype(jnp.int32)
        s = s + jnp.concatenate([bias_ref[first, 2 * hp], bias_ref[first, 2 * hp + 1]], axis=0)
        m = jnp.max(s, axis=-1, keepdims=True)
        p = jnp.exp2(s - m)
        den = jnp.sum(p, axis=-1, keepdims=True)
        pv = jnp.dot(p.astype(BF16), vw, preferred_element_type=F32)
        lse = m + jnp.log2(den)
        inv = 1.0 / den
        o2 = jnp.where(low, pv[0:blk] * inv[0:blk], pv[blk:2 * blk] * inv[blk:2 * blk])
        if has_prev or scatter:
            lse2 = jnp.where(low, lse[0:blk], lse[blk:2 * blk])
        if has_prev:
            po = po_ref[pl.ds(r0, blk), pl.ds(c0, V7X_LANES)].astype(F32)
            plse = pl_ref[pl.ds(r0, blk), pl.ds(c0, V7X_LANES)]
            top = jnp.maximum(plse, lse2)
            wa = jnp.exp2(plse - top)
            wb = jnp.exp2(lse2 - top)
            tot = wa + wb
            o2 = (wa * po + wb * o2) / tot
            lse2 = top + jnp.log2(tot)
        if not scatter:
            o_ref[pl.ds(r0, blk), pl.ds(c0, V7X_LANES)] = o2.astype(o_ref.dtype)
            return
        so_ref[slot] = o2
        sl_ref[slot] = lse2
        d0 = pl.multiple_of(qb * sub, sub)
        for jm in range(step):
            o_ref[jm, pl.ds(d0, sub), pl.ds(c0, V7X_LANES)] = so_ref[
                slot, pl.ds(jm, sub, stride=step), :].astype(o_ref.dtype)
            l_ref[jm, pl.ds(d0, sub), pl.ds(c0, V7X_LANES)] = sl_ref[slot, pl.ds(jm, sub, stride=step), :]

    per_iter = ATTN_UNITS_PER_ITER

    def body(i, carry):
        for slot in range(per_iter):
            unit(i * per_iter + slot, slot)
        return carry

    lax.fori_loop(0, (rows // blk) * n_pairs // per_iter, body, 0)


def _attention_pattern(index, srcs, bias, prev, batch, seq):
    dilation = DILATED_PATTERNS[index][1]
    last = index == len(DILATED_PATTERNS) - 1
    length = seq // dilation
    rows = min(ATTN_ROWS, length)
    blk = ATTN_BLOCK
    width = ATTN_WIDTH
    per_step = rows // blk

    if index == 0:
        src = srcs.reshape(batch, seq, IN_WIDTH)
        cols = QKV_TILES

        def cur(c):
            return pl.BlockSpec((None, rows, width), lambda b, r, t: (b, t, c))

        def prv(c):
            return pl.BlockSpec((None, blk, width), lambda b, r, t: (b, jnp.maximum(t * per_step - 1, 0), c))
    else:
        src = srcs
        cols = (0, 1, 2)

        def cur(c):
            return pl.BlockSpec((None, None, rows, width), lambda b, r, t: (b, r, t, c))

        def prv(c):
            return pl.BlockSpec((None, None, blk, width),
                                lambda b, r, t: (b, r, jnp.maximum(t * per_step - 1, 0), c))

    res_blk = pl.BlockSpec((None, None, rows, width), lambda b, r, t: (b, r, t, 0))
    in_specs = [cur(cols[0]), cur(cols[1]), prv(cols[1]), cur(cols[2]), prv(cols[2]),
                pl.BlockSpec(bias.shape, lambda b, r, t: (0, 0, 0, 0))]
    args = [src, src, src, src, src, bias]
    if prev is not None:
        in_specs += [res_blk, res_blk]
        args += [prev[0].reshape(batch, dilation, length, width), prev[1].reshape(batch, dilation, length, width)]
    scratch = [pltpu.VMEM((blk + rows, width), BF16), pltpu.VMEM((blk + rows, width), BF16)]
    if last:
        out_specs = [res_blk]
        out_shape = [jax.ShapeDtypeStruct((batch, dilation, length, width), BF16)]
    else:
        step = DILATION_STEP
        shape = (batch, step, dilation, length // step, width)
        out_blk = pl.BlockSpec((None, step, None, rows // step, width), lambda b, r, t: (b, 0, r, t, 0))
        out_specs = [out_blk, out_blk]
        out_shape = [jax.ShapeDtypeStruct(shape, BF16), jax.ShapeDtypeStruct(shape, F32)]
        scratch += [pltpu.VMEM((ATTN_UNITS_PER_ITER, blk, V7X_LANES), F32)] * 2
    return pl.pallas_call(
        functools.partial(_attn_body, has_prev=prev is not None, scatter=not last, rows=rows),
        grid=(batch, dilation, length // rows),
        in_specs=in_specs,
        out_specs=out_specs,
        out_shape=out_shape,
        scratch_shapes=scratch,
        compiler_params=_cparams(("parallel", "parallel", "arbitrary")),
        name=f"dilated_attention_d{dilation}",
    )(*args)


def _t5_bucket(distance):
    max_exact = REL_BUCKETS // 2
    scaled = jnp.log(jnp.maximum(distance, max_exact).astype(F32) / max_exact) / math.log(
        REL_MAX_DISTANCE / max_exact)
    large = jnp.minimum(max_exact + (scaled * (REL_BUCKETS - max_exact)).astype(jnp.int32), REL_BUCKETS - 1)
    return jnp.where(distance < max_exact, distance, large)


def _band_bias(rel_bias, window, dilation):
    blk = ATTN_BLOCK
    n_back = window // dilation
    qi = jnp.arange(blk)[:, None]
    kj = jnp.arange(2 * blk)[None, :]
    delta = blk + qi - kj
    band = (delta >= 0) & (delta <= n_back)
    onehot = jax.nn.one_hot(_t5_bucket(jnp.maximum(delta, 0) * dilation), REL_BUCKETS, dtype=F32)
    bias = jnp.einsum("qkb,bh->hqk", onehot, rel_bias.astype(F32) * LOG2E, precision=lax.Precision.HIGHEST)
    bias = jnp.where(band[None], bias, MASKED)
    return jnp.stack([bias, jnp.where(kj[None] >= blk, bias, MASKED)])


def _outproj_even_body(gb_ref, gc_ref, xa_ref, z_ref, oa_ref, gch_ref, xah_ref, cw_ref, w_ref, x_ref,
                       o_ref, slab_ref, *, tiles_per_seq):
    i = pl.program_id(0)
    tm = gc_ref.shape[0]
    dil = oa_ref.shape[0]
    for r in range(dil):
        for s in range(N_SLABS):
            slab_ref[s, pl.ds(r, tm // dil, stride=dil), :] = oa_ref[
                r, :, s * V7X_LANES:(s + 1) * V7X_LANES].astype(F32)
    attn = jnp.concatenate([slab_ref[s] for s in range(N_SLABS)], axis=1)

    u = gc_ref[...].astype(F32) * xa_ref[...].astype(F32)
    halo = gch_ref[...].astype(F32) * xah_ref[...].astype(F32)
    halo = jnp.where((i % tiles_per_seq) == 0, 0.0, halo)
    row = lax.broadcasted_iota(jnp.int32, u.shape, 0)
    h6 = halo[V7X_SUBLANES - 2:V7X_SUBLANES - 1]
    h7 = halo[V7X_SUBLANES - 1:V7X_SUBLANES]
    u1 = jnp.where(row >= 1, pltpu.roll(u, 1, axis=0), h7)
    u2 = jnp.where(row >= 2, pltpu.roll(u, 2, axis=0), jnp.where(row == 0, h6, h7))
    cw = cw_ref[...]
    conv = cw[0:1] * u2 + cw[1:2] * u1 + cw[2:3] * u
    gate = _silu(z_ref[...].astype(F32))
    half = CONV_WIDTH
    ya = (gb_ref[...].astype(F32) * conv * gate[:, :half]).astype(BF16)
    yb = (attn * gate[:, half:]).astype(BF16)
    acc = jnp.dot(ya, w_ref[0:half, :], preferred_element_type=F32)
    acc += jnp.dot(yb, w_ref[half:, :], preferred_element_type=F32)
    o_ref[...] = x_ref[...] + acc


def _outproj_even(proj, o_attn, conv_w, w_bf16, x2d, seq):
    t, d = x2d.shape
    tm = PROJ_TM
    w = CONV_WIDTH
    halo_blocks = tm // V7X_SUBLANES
    tiles_per_seq = seq // tm
    dil = o_attn.shape[1]

    def colblk(c, width=w):
        return pl.BlockSpec((tm, width), lambda i: (i, c))

    def halo(c):
        return pl.BlockSpec((V7X_SUBLANES, w), lambda i: (jnp.maximum(i * halo_blocks - 1, 0), c))

    return pl.pallas_call(
        functools.partial(_outproj_even_body, tiles_per_seq=tiles_per_seq),
        grid=(t // tm,),
        in_specs=[colblk(0), colblk(1), colblk(2), colblk(3, MIX_WIDTH),
                  pl.BlockSpec((None, dil, tm // dil, w), lambda i: (i // tiles_per_seq, 0, i % tiles_per_seq, 0)),
                  halo(1), halo(2),
                  pl.BlockSpec(conv_w.shape, lambda i: (0, 0)),
                  pl.BlockSpec(w_bf16.shape, lambda i: (0, 0)),
                  pl.BlockSpec((tm, d), lambda i: (i, 0))],
        out_specs=pl.BlockSpec((tm, d), lambda i: (i, 0)),
        out_shape=jax.ShapeDtypeStruct((t, d), F32),
        scratch_shapes=[pltpu.VMEM((N_SLABS, tm, V7X_LANES), F32)],
        compiler_params=_cparams(("parallel",)),
        name="conv_gate_outproj",
    )(proj, proj, proj, proj, o_attn, proj, proj, conv_w.astype(F32), w_bf16, x2d)


def _cumsum_rows(g):
    sub = V7X_SUBLANES
    row = lax.broadcasted_iota(jnp.int32, (sub, g.shape[1]), 0)
    out = []
    run = None
    for i in range(g.shape[0] // sub):
        x = g[i * sub:(i + 1) * sub]
        for s in (1, 2, 4):
            x = x + jnp.where(row >= s, pltpu.roll(x, s, axis=0), 0.0)
        if run is not None:
            x = x + run
        run = x[sub - 1:sub]
        out.append(x)
    return jnp.concatenate(out, axis=0)


def _hgrn_gates(f_pre, lb_terms):
    e = jnp.exp(-jnp.abs(f_pre))
    log_sig = jnp.minimum(f_pre, 0.0) - jnp.log(1.0 + e)
    sig_neg = jnp.where(f_pre >= 0.0, e, 1.0) / (1.0 + e)
    if lb_terms is None:
        return log_sig, sig_neg
    log_lb, log1m_lb, one_m_lb = lb_terms
    b = log1m_lb + log_sig
    hi = jnp.maximum(log_lb, b)
    log_f = hi + jnp.log(1.0 + jnp.exp(-jnp.abs(log_lb - b)))
    return log_f, one_m_lb * sig_neg


def _hgrn_body(q_ref, f_ref, i_ref, z_ref, llb_ref, l1m_ref, oml_ref, og_ref, y_ref,
               st_ref, st0_ref, qrows_ref, frows_ref, krows_ref, *, rows, zero_lb):
    c = HGRN_CHUNK
    t = pl.program_id(2)

    @pl.when(t == 0)
    def _():
        st_ref[...] = jnp.zeros_like(st_ref)

    st0_ref[...] = st_ref[...]
    lb_terms = None if zero_lb else (llb_ref[...], l1m_ref[...], oml_ref[...])
    o_gain = og_ref[...]
    ri = lax.broadcasted_iota(jnp.int32, (c, c), 0)
    ci = lax.broadcasted_iota(jnp.int32, (c, c), 1)
    causal = ri >= ci

    def finish(o, sl):
        ms = jnp.mean(o * o, axis=-1, keepdims=True)
        y = o * lax.rsqrt(ms + EPS) * o_gain * _silu(z_ref[0, sl, :].astype(F32))
        y_ref[0, sl, :] = y.astype(y_ref.dtype)

    worst = jnp.zeros((1, HGRN_DIM), F32)
    for n in range(rows // c):
        sl = slice(n * c, (n + 1) * c)
        q = _silu(q_ref[0, sl, :].astype(F32))
        log_f, k = _hgrn_gates(f_ref[0, sl, :].astype(F32), lb_terms)
        v = i_ref[0, sl, :]
        gcum = _cumsum_rows(log_f)
        g_last = gcum[c - 1:c]
        rho = 0.5 * g_last
        worst = jnp.minimum(worst, rho)
        qt = (q * jnp.exp(gcum - rho)).astype(BF16)
        kh = k * jnp.exp(rho - gcum)
        e_rho = jnp.exp(rho)
        st = st_ref[...]
        rhs = jnp.concatenate([kh.astype(BF16), (st * e_rho).astype(BF16)], axis=0)
        so = lax.dot_general(qt, rhs, (((1,), (1,)), ((), ())), preferred_element_type=F32)
        scores = jnp.where(causal, so[:, :c], 0.0).astype(BF16)
        o = so[:, c:] + jnp.dot(scores, v, preferred_element_type=F32)
        kd = (kh * e_rho).astype(BF16)
        st_ref[...] = st * jnp.exp(g_last) + lax.dot_general(
            v, kd, (((0,), (0,)), ((), ())), preferred_element_type=F32)
        finish(o, sl)

    @pl.when(jnp.min(worst) < -HGRN_MAX_HALF_DECAY)
    def _():
        lane = lax.broadcasted_iota(jnp.int32, (HGRN_DIM, c), 1)
        for n in range(rows // c):
            sl = slice(n * c, (n + 1) * c)
            log_f, k = _hgrn_gates(f_ref[0, sl, :].astype(F32), lb_terms)
            qrows_ref[...] = _silu(q_ref[0, sl, :].astype(F32))
            frows_ref[...] = jnp.exp(log_f)
            krows_ref[...] = k
            vt = jnp.transpose(i_ref[0, sl, :].astype(F32))

            def step(s, carry):
                st, ot = carry
                vcol = jnp.sum(jnp.where(lane == s, vt, 0.0), axis=-1, keepdims=True)
                st = st * frows_ref[pl.ds(s, 1), :] + vcol * krows_ref[pl.ds(s, 1), :]
                ocol = jnp.sum(st * qrows_ref[pl.ds(s, 1), :], axis=-1, keepdims=True)
                ot = jnp.where(lane == s, ocol, ot)
                return st, ot

            st, ot = lax.fori_loop(0, c, step, (st0_ref[...], jnp.zeros((HGRN_DIM, c), F32)))
            st0_ref[...] = st
            finish(jnp.transpose(ot), sl)
        st_ref[...] = st0_ref[...]


def _hgrn2(proj, lb, zero_lb, o_gain, batch, seq):
    rows = HGRN_ROWS
    hd = HGRN_DIM
    projv = proj.reshape(batch, seq, IN_WIDTH)
    per_kind = HGRN_HEADS

    def part(kind):
        return pl.BlockSpec((1, rows, hd), lambda b, h, t: (b, t, kind * per_kind + h))

    def per_head():
        return pl.BlockSpec((1, hd), lambda b, h, t: (0, h))

    lb = lb.reshape(1, -1).astype(F32)
    y = pl.pallas_call(
        functools.partial(_hgrn_body, rows=rows, zero_lb=zero_lb),
        grid=(batch, HGRN_HEADS, seq // rows),
        in_specs=[part(0), part(1), part(2), part(3), per_head(), per_head(), per_head(), per_head()],
        out_specs=pl.BlockSpec((1, rows, hd), lambda b, h, t: (b, t, h)),
        out_shape=jax.ShapeDtypeStruct((batch, seq, MIX_WIDTH), BF16),
        scratch_shapes=[pltpu.VMEM((hd, hd), F32), pltpu.VMEM((hd, hd), F32)]
        + [pltpu.VMEM((HGRN_CHUNK, hd), F32)] * 3,
        compiler_params=_cparams(("parallel", "parallel", "arbitrary")),
        name="hgrn2_recurrence",
    )(projv, projv, projv, projv, jnp.log(lb), jnp.log1p(-lb), 1.0 - lb, o_gain.reshape(1, -1).astype(F32))
    return y.reshape(batch * seq, MIX_WIDTH)


def _outproj_body(y_ref, w_ref, x_ref, o_ref):
    o_ref[...] = x_ref[...] + jnp.dot(y_ref[...], w_ref[...], preferred_element_type=F32)


def _outproj(y, w_bf16, x2d):
    t, d = x2d.shape
    tm = PROJ_TM
    return pl.pallas_call(
        _outproj_body,
        grid=(t // tm,),
        in_specs=[pl.BlockSpec((tm, y.shape[1]), lambda i: (i, 0)),
                  pl.BlockSpec(w_bf16.shape, lambda i: (0, 0)),
                  pl.BlockSpec((tm, d), lambda i: (i, 0))],
        out_specs=pl.BlockSpec((tm, d), lambda i: (i, 0)),
        out_shape=jax.ShapeDtypeStruct((t, d), F32),
        compiler_params=_cparams(("parallel",)),
        name="outproj_residual",
    )(y, w_bf16, x2d)


def kernel(x, ln_even, w_in_even, conv_w, q_gain, k_gain, w_out_even, rel_bias, ln_odd, w_in_odd,
           lower_bounds, o_gain, w_out_odd):
    batch, seq, d = x.shape
    assert d == D_MODEL and seq % (DILATED_PATTERNS[-1][1] * ATTN_BLOCK) == 0 and seq % PROJ_TM == 0
    depth = ln_even.shape[0] + ln_odd.shape[0]
    x2d = x.reshape(batch * seq, d).astype(F32)

    lbs = jnp.cumsum(jax.nn.softmax(lower_bounds.astype(F32), axis=0), axis=0)
    lbs = lbs - lbs[0:1]
    biases = [_band_bias(rel_bias, window, dilation) for window, dilation in DILATED_PATTERNS]
    ones = jnp.ones((1, IN_WIDTH), F32)
    q_tile, k_tile, _ = QKV_TILES

    for layer in range(depth):
        j = layer // 2
        if layer % 2 == 0:
            q_scale = jnp.tile(q_gain[j].astype(F32), ATTN_HEADS) * (ATTN_HEAD_DIM ** -0.5 * LOG2E)
            colscale = ones.at[0, q_tile * PROJ_TN:(q_tile + 1) * PROJ_TN].set(q_scale)
            colscale = colscale.at[0, k_tile * PROJ_TN:(k_tile + 1) * PROJ_TN].set(
                jnp.tile(k_gain[j].astype(F32), ATTN_HEADS))
            proj, qkv4, qkv16 = _norm_inproj(x2d, ln_even[j], w_in_even[j].astype(BF16), colscale, True,
                                             batch, seq)
            prev = None
            for index, srcs in enumerate((proj, qkv4, qkv16)):
                prev = _attention_pattern(index, srcs, biases[index], prev, batch, seq)
            x2d = _outproj_even(proj, prev[0], conv_w[j], w_out_even[j].astype(BF16), x2d, seq)
        else:
            (proj,) = _norm_inproj(x2d, ln_odd[j], w_in_odd[j].astype(BF16), ones, False, batch, seq)
            y = _hgrn2(proj, lbs[j], j == 0, o_gain[j], batch, seq)
            x2d = _outproj(y, w_out_odd[j].astype(BF16), x2d)
    return x2d.reshape(batch, seq, d).astype(x.dtype)
```

```python
import functools
import math

import jax
import jax.numpy as jnp
import numpy as np
from jax import lax
from jax.experimental import pallas as pl
from jax.experimental.pallas import tpu as pltpu

F32 = jnp.float32
BF16 = jnp.bfloat16

D_MODEL = 1024
MIX_WIDTH = 2 * D_MODEL
CONV_WIDTH = MIX_WIDTH // 2
ATTN_HEAD_DIM = 64
ATTN_HEADS = 16
ATTN_WIDTH = ATTN_HEADS * ATTN_HEAD_DIM
ATTN_BLOCK = 128
DILATED_PATTERNS = ((128, 1), (512, 4), (2048, 16))
DILATION_STEP = 4
REL_BUCKETS = 32
REL_MAX_DISTANCE = 2048
HGRN_HEADS = 16
HGRN_DIM = 128
IN_WIDTH = 8192
QKV_TILES = (3, 4, 5)
EPS = 1e-6
MASKED = -1e30
LOG2E = math.log2(math.e)

V7X_LANES = 128
V7X_SUBLANES = 8
V7X_VMEM_LIMIT_BYTES = 56 * 1024 * 1024

PROJ_TM = 512
INPROJ_TM = 256
PROJ_TN = 1024
ATTN_ROWS = 512
ATTN_UNITS_PER_ITER = 8
HGRN_ROWS = 512
HGRN_CHUNK = 128
HGRN_HEADS_PER_STEP = 2
HGRN_MAX_HALF_DECAY = 100.0

N_SLABS = ATTN_WIDTH // V7X_LANES


def _cparams(semantics):
    return pltpu.CompilerParams(dimension_semantics=semantics, vmem_limit_bytes=V7X_VMEM_LIMIT_BYTES)


def _silu(t):
    half = 0.5 * t
    return half + half * jnp.tanh(half)


EVEN_TILE_KINDS = ("plain", "plain", "plain", "headnorm", "headnorm", "scatter", "silu", "silu")
ODD_TILE_KINDS = ("silu", "silu", "log2gate", "log2gate", "plain", "plain", "silu", "silu")


def _log2_forget_gate(f_pre, lb_terms):
    log_f = jnp.minimum(f_pre, 0.0) - jnp.log(1.0 + jnp.exp(-jnp.abs(f_pre)))
    if lb_terms is not None:
        log_lb, log1m_lb = lb_terms
        b = log1m_lb + log_f
        log_f = jnp.maximum(log_lb, b) + jnp.log(1.0 + jnp.exp(-jnp.abs(log_lb - b)))
    return log_f * LOG2E


def _inproj_body(x_ref, g_ref, w_ref, ca_ref, cb_ref, bd_ref, *rest, kinds, zero_lb):
    scatter = "scatter" in kinds
    if scatter:
        o_ref, o4_ref, o16_ref, slab_ref = rest
    else:
        (o_ref,) = rest
    x = x_ref[...]
    ms = jnp.mean(x * x, axis=-1, keepdims=True)
    hn = (x * lax.rsqrt(ms + EPS) * g_ref[...]).astype(BF16)
    tm = x.shape[0]
    tn = PROJ_TN

    def emit_residue_major(res, cols, part):
        o_ref[:, cols] = res.astype(o_ref.dtype)
        for s in range(N_SLABS):
            slab_ref[part, s] = res[:, s * V7X_LANES:(s + 1) * V7X_LANES]
        for dil, ref in ((DILATED_PATTERNS[1][1], o4_ref), (DILATED_PATTERNS[2][1], o16_ref)):
            for r in range(dil):
                for s in range(N_SLABS):
                    lanes = slice(part * tn + s * V7X_LANES, part * tn + (s + 1) * V7X_LANES)
                    ref[r, :, lanes] = slab_ref[part, s, pl.ds(r, tm // dil, stride=dil), :].astype(ref.dtype)

    for jt, kind in enumerate(kinds):
        cols = slice(jt * tn, (jt + 1) * tn)
        acc = jnp.dot(hn, w_ref[:, cols], preferred_element_type=F32)
        if kind == "plain":
            o_ref[:, cols] = acc.astype(o_ref.dtype)
        elif kind == "silu":
            o_ref[:, cols] = _silu(acc).astype(o_ref.dtype)
        elif kind == "log2gate":
            lb_terms = None if zero_lb else (ca_ref[:, cols], cb_ref[:, cols])
            o_ref[:, cols] = _log2_forget_gate(acc, lb_terms).astype(o_ref.dtype)
        elif kind == "scatter":
            emit_residue_major(acc, cols, jt - QKV_TILES[0])
        else:
            width = bd_ref.shape[0]
            parts = []
            for s in range(tn // width):
                a = acc[:, s * width:(s + 1) * width]
                ms = jnp.dot((a * a).astype(BF16), bd_ref[...], preferred_element_type=F32)
                parts.append(a * lax.rsqrt(ms + EPS) * ca_ref[:, jt * tn + s * width:jt * tn + (s + 1) * width])
            emit_residue_major(jnp.concatenate(parts, axis=1), cols, jt - QKV_TILES[0])


def _norm_inproj(x2d, gain, w_bf16, col_a, col_b, kinds, zero_lb, batch, seq):
    t, d = x2d.shape
    n = w_bf16.shape[1]
    tm, tn = INPROJ_TM, PROJ_TN
    assert len(kinds) == n // tn
    width = 2 * V7X_LANES
    head_of_lane = np.arange(width) // ATTN_HEAD_DIM
    bd = jnp.asarray((head_of_lane[:, None] == head_of_lane[None, :]) / ATTN_HEAD_DIM, BF16)
    tiles_per_seq = seq // tm
    out_specs = [pl.BlockSpec((tm, n), lambda i: (i, 0))]
    out_shape = [jax.ShapeDtypeStruct((t, n), BF16)]
    scratch = []
    if "scatter" in kinds:
        qkv_width = len(QKV_TILES) * tn
        for _, dil in DILATED_PATTERNS[1:]:
            out_specs.append(pl.BlockSpec((None, dil, tm // dil, qkv_width),
                                          lambda i: (i // tiles_per_seq, 0, i % tiles_per_seq, 0)))
            out_shape.append(jax.ShapeDtypeStruct((batch, dil, seq // dil, qkv_width), BF16))
        scratch.append(pltpu.VMEM((len(QKV_TILES), N_SLABS, tm, V7X_LANES), F32))
    resident = dict(pipeline_mode=pl.Buffered(1))
    return pl.pallas_call(
        functools.partial(_inproj_body, kinds=kinds, zero_lb=zero_lb),
        grid=(t // tm,),
        in_specs=[
            pl.BlockSpec((tm, d), lambda i: (i, 0)),
            pl.BlockSpec((1, d), lambda i: (0, 0)),
            pl.BlockSpec((d, n), lambda i: (0, 0), **resident),
            pl.BlockSpec((1, n), lambda i: (0, 0)),
            pl.BlockSpec((1, n), lambda i: (0, 0)),
            pl.BlockSpec((width, width), lambda i: (0, 0)),
        ],
        out_specs=out_specs,
        out_shape=out_shape,
        scratch_shapes=scratch,
        compiler_params=_cparams(("parallel",)),
        name="norm_inproj_even" if "scatter" in kinds else "norm_inproj_odd",
    )(x2d, gain.reshape(1, d).astype(F32), w_bf16, col_a, col_b, bd)


def _attn_body(*refs, has_prev, scatter, rows):
    refs = list(refs)
    q_ref, kc_ref, kp_ref, vc_ref, vp_ref, bias_ref = refs[:6]
    del refs[:6]
    if has_prev:
        po_ref, pl_ref = refs[:2]
        del refs[:2]
    if scatter:
        o_ref, l_ref, kw_ref, vw_ref, so_ref, sl_ref = refs
    else:
        o_ref, kw_ref, vw_ref = refs
    blk = ATTN_BLOCK
    t = pl.program_id(2)
    kw_ref[0:blk, :] = kp_ref[...]
    kw_ref[blk:blk + rows, :] = kc_ref[...]
    vw_ref[0:blk, :] = vp_ref[...]
    vw_ref[blk:blk + rows, :] = vc_ref[...]

    lane = lax.broadcasted_iota(jnp.int32, (blk, V7X_LANES), 1)
    low = lane < ATTN_HEAD_DIM
    low_bf = jnp.where(low, 1.0, 0.0).astype(BF16)
    high_bf = jnp.where(low, 0.0, 1.0).astype(BF16)
    low2_bf = jnp.concatenate([low_bf, low_bf], axis=0)
    high2_bf = jnp.concatenate([high_bf, high_bf], axis=0)
    n_pairs = ATTN_HEADS // 2
    step = DILATION_STEP
    sub = blk // step

    def locate(idx):
        qb = idx // n_pairs
        hp = idx % n_pairs
        return qb, hp, pl.multiple_of(qb * blk, blk), pl.multiple_of(hp * V7X_LANES, V7X_LANES)

    def logits(idx):
        qb, hp, r0, c0 = locate(idx)
        q2 = q_ref[pl.ds(r0, blk), pl.ds(c0, V7X_LANES)]
        kw = kw_ref[pl.ds(r0, 2 * blk), pl.ds(c0, V7X_LANES)]
        qs = jnp.concatenate([q2 * low_bf, q2 * high_bf], axis=0)
        return lax.dot_general(qs, kw, (((1,), (1,)), ((), ())), preferred_element_type=F32)

    def softmax(idx, s):
        qb, hp, _, _ = locate(idx)
        first = jnp.logical_and(t == 0, qb == 0).astype(jnp.int32)
        s = s + jnp.concatenate([bias_ref[first, 2 * hp], bias_ref[first, 2 * hp + 1]], axis=0)
        m = jnp.max(s, axis=-1, keepdims=True)
        p = jnp.exp2(s - m).astype(BF16)
        return p, jnp.where(low, m[0:blk], m[blk:2 * blk])

    def weighted(idx, p):
        _, _, r0, c0 = locate(idx)
        vw = vw_ref[pl.ds(r0, 2 * blk), pl.ds(c0, V7X_LANES)]
        rhs = jnp.concatenate([jnp.concatenate([vw * low2_bf, low2_bf], axis=1),
                               jnp.concatenate([vw * high2_bf, high2_bf], axis=1)], axis=0)
        lhs = jnp.concatenate([p[0:blk], p[blk:2 * blk]], axis=1)
        return jnp.dot(lhs, rhs, preferred_element_type=F32)

    def finish(idx, slot, pv, m2):
        qb, _, r0, c0 = locate(idx)
        den = pv[:, V7X_LANES:]
        o2 = pv[:, 0:V7X_LANES] / den
        if has_prev or scatter:
            lse2 = m2 + jnp.log(den) * LOG2E
        if has_prev:
            po = po_ref[pl.ds(r0, blk), pl.ds(c0, V7X_LANES)].astype(F32)
            plse = pl_ref[pl.ds(r0, blk), pl.ds(c0, V7X_LANES)]
            gap = plse - lse2
            e = jnp.exp2(-jnp.abs(gap))
            prev_larger = gap >= 0.0
            tot = 1.0 + e
            o2 = (jnp.where(prev_larger, po, o2) + e * jnp.where(prev_larger, o2, po)) / tot
            lse2 = jnp.maximum(plse, lse2) + jnp.log(tot) * LOG2E
        if not scatter:
            o_ref[pl.ds(r0, blk), pl.ds(c0, V7X_LANES)] = o2.astype(o_ref.dtype)
            return
        so_ref[slot] = o2
        sl_ref[slot] = lse2
        d0 = pl.multiple_of(qb * sub, sub)
        for jm in range(step):
            o_ref[jm, pl.ds(d0, sub), pl.ds(c0, V7X_LANES)] = so_ref[
                slot, pl.ds(jm, sub, stride=step), :].astype(o_ref.dtype)
            l_ref[jm, pl.ds(d0, sub), pl.ds(c0, V7X_LANES)] = sl_ref[slot, pl.ds(jm, sub, stride=step), :]

    per_iter = ATTN_UNITS_PER_ITER

    def body(i, carry):
        units = [i * per_iter + slot for slot in range(per_iter)]
        scores = [logits(u) for u in units]
        probs = [softmax(u, s) for u, s in zip(units, scores)]
        pvs = [weighted(u, p) for u, (p, _) in zip(units, probs)]
        for slot, (u, pv, (_, m2)) in enumerate(zip(units, pvs, probs)):
            finish(u, slot, pv, m2)
        return carry

    lax.fori_loop(0, (rows // blk) * n_pairs // per_iter, body, 0)


def _attention_pattern(index, srcs, bias, prev, batch, seq):
    dilation = DILATED_PATTERNS[index][1]
    last = index == len(DILATED_PATTERNS) - 1
    length = seq // dilation
    rows = min(ATTN_ROWS, length)
    blk = ATTN_BLOCK
    width = ATTN_WIDTH
    per_step = rows // blk

    if index == 0:
        src = srcs.reshape(batch, seq, IN_WIDTH)
        cols = QKV_TILES

        def cur(c):
            return pl.BlockSpec((None, rows, width), lambda b, r, t: (b, t, c))

        def prv(c):
            return pl.BlockSpec((None, blk, width), lambda b, r, t: (b, jnp.maximum(t * per_step - 1, 0), c))
    else:
        src = srcs
        cols = (0, 1, 2)

        def cur(c):
            return pl.BlockSpec((None, None, rows, width), lambda b, r, t: (b, r, t, c))

        def prv(c):
            return pl.BlockSpec((None, None, blk, width),
                                lambda b, r, t: (b, r, jnp.maximum(t * per_step - 1, 0), c))

    res_blk = pl.BlockSpec((None, None, rows, width), lambda b, r, t: (b, r, t, 0))
    in_specs = [cur(cols[0]), cur(cols[1]), prv(cols[1]), cur(cols[2]), prv(cols[2]),
                pl.BlockSpec(bias.shape, lambda b, r, t: (0, 0, 0, 0))]
    args = [src, src, src, src, src, bias]
    if prev is not None:
        in_specs += [res_blk, res_blk]
        args += [prev[0].reshape(batch, dilation, length, width), prev[1].reshape(batch, dilation, length, width)]
    scratch = [pltpu.VMEM((blk + rows, width), BF16), pltpu.VMEM((blk + rows, width), BF16)]
    if last:
        out_specs = [res_blk]
        out_shape = [jax.ShapeDtypeStruct((batch, dilation, length, width), BF16)]
    else:
        step = DILATION_STEP
        shape = (batch, step, dilation, length // step, width)
        out_blk = pl.BlockSpec((None, step, None, rows // step, width), lambda b, r, t: (b, 0, r, t, 0))
        out_specs = [out_blk, out_blk]
        out_shape = [jax.ShapeDtypeStruct(shape, BF16), jax.ShapeDtypeStruct(shape, F32)]
        scratch += [pltpu.VMEM((ATTN_UNITS_PER_ITER, blk, V7X_LANES), F32)] * 2
    return pl.pallas_call(
        functools.partial(_attn_body, has_prev=prev is not None, scatter=not last, rows=rows),
        grid=(batch, dilation, length // rows),
        in_specs=in_specs,
        out_specs=out_specs,
        out_shape=out_shape,
        scratch_shapes=scratch,
        compiler_params=_cparams(("parallel", "parallel", "arbitrary")),
        name=f"dilated_attention_d{dilation}",
    )(*args)


def _t5_bucket(distance):
    max_exact = REL_BUCKETS // 2
    scaled = jnp.log(jnp.maximum(distance, max_exact).astype(F32) / max_exact) / math.log(
        REL_MAX_DISTANCE / max_exact)
    large = jnp.minimum(max_exact + (scaled * (REL_BUCKETS - max_exact)).astype(jnp.int32), REL_BUCKETS - 1)
    return jnp.where(distance < max_exact, distance, large)


def _band_bias(rel_bias, window, dilation):
    blk = ATTN_BLOCK
    n_back = window // dilation
    qi = jnp.arange(blk)[:, None]
    kj = jnp.arange(2 * blk)[None, :]
    delta = blk + qi - kj
    band = (delta >= 0) & (delta <= n_back)
    onehot = jax.nn.one_hot(_t5_bucket(jnp.maximum(delta, 0) * dilation), REL_BUCKETS, dtype=F32)
    bias = jnp.einsum("qkb,bh->hqk", onehot, rel_bias.astype(F32) * LOG2E, precision=lax.Precision.HIGHEST)
    bias = jnp.where(band[None], bias, MASKED)
    return jnp.stack([bias, jnp.where(kj[None] >= blk, bias, MASKED)])


def _outproj_even_body(gb_ref, gc_ref, xa_ref, z_ref, oa_ref, gch_ref, xah_ref, cw_ref, w_ref, x_ref,
                       o_ref, slab_ref, *, tiles_per_seq):
    i = pl.program_id(0)
    tm = gc_ref.shape[0]
    dil = oa_ref.shape[0]
    for r in range(dil):
        for s in range(N_SLABS):
            slab_ref[s, pl.ds(r, tm // dil, stride=dil), :] = oa_ref[
                r, :, s * V7X_LANES:(s + 1) * V7X_LANES].astype(F32)
    attn = jnp.concatenate([slab_ref[s] for s in range(N_SLABS)], axis=1)

    u = gc_ref[...].astype(F32) * xa_ref[...].astype(F32)
    halo = gch_ref[...].astype(F32) * xah_ref[...].astype(F32)
    halo = jnp.where((i % tiles_per_seq) == 0, 0.0, halo)
    row = lax.broadcasted_iota(jnp.int32, u.shape, 0)
    h6 = halo[V7X_SUBLANES - 2:V7X_SUBLANES - 1]
    h7 = halo[V7X_SUBLANES - 1:V7X_SUBLANES]
    u1 = jnp.where(row >= 1, pltpu.roll(u, 1, axis=0), h7)
    u2 = jnp.where(row >= 2, pltpu.roll(u, 2, axis=0), jnp.where(row == 0, h6, h7))
    cw = cw_ref[...]
    conv = cw[0:1] * u2 + cw[1:2] * u1 + cw[2:3] * u
    gate = z_ref[...].astype(F32)
    half = CONV_WIDTH
    ya = (gb_ref[...].astype(F32) * conv * gate[:, :half]).astype(BF16)
    yb = (attn * gate[:, half:]).astype(BF16)
    acc = jnp.dot(ya, w_ref[0:half, :], preferred_element_type=F32)
    acc += jnp.dot(yb, w_ref[half:, :], preferred_element_type=F32)
    o_ref[...] = x_ref[...] + acc


def _outproj_even(proj, o_attn, conv_w, w_bf16, x2d, seq):
    t, d = x2d.shape
    tm = PROJ_TM
    w = CONV_WIDTH
    halo_blocks = tm // V7X_SUBLANES
    tiles_per_seq = seq // tm
    dil = o_attn.shape[1]

    def colblk(c, width=w):
        return pl.BlockSpec((tm, width), lambda i: (i, c))

    def halo(c):
        return pl.BlockSpec((V7X_SUBLANES, w), lambda i: (jnp.maximum(i * halo_blocks - 1, 0), c))

    return pl.pallas_call(
        functools.partial(_outproj_even_body, tiles_per_seq=tiles_per_seq),
        grid=(t // tm,),
        in_specs=[colblk(0), colblk(1), colblk(2), colblk(3, MIX_WIDTH),
                  pl.BlockSpec((None, dil, tm // dil, w), lambda i: (i // tiles_per_seq, 0, i % tiles_per_seq, 0)),
                  halo(1), halo(2),
                  pl.BlockSpec(conv_w.shape, lambda i: (0, 0)),
                  pl.BlockSpec(w_bf16.shape, lambda i: (0, 0)),
                  pl.BlockSpec((tm, d), lambda i: (i, 0))],
        out_specs=pl.BlockSpec((tm, d), lambda i: (i, 0)),
        out_shape=jax.ShapeDtypeStruct((t, d), F32),
        scratch_shapes=[pltpu.VMEM((N_SLABS, tm, V7X_LANES), F32)],
        compiler_params=_cparams(("parallel",)),
        name="conv_gate_outproj",
    )(proj, proj, proj, proj, o_attn, proj, proj, conv_w.astype(F32), w_bf16, x2d)


def _cumsum_rows(g):
    sub = V7X_SUBLANES
    row = lax.broadcasted_iota(jnp.int32, (sub, g.shape[1]), 0)
    out = []
    run = None
    for i in range(g.shape[0] // sub):
        x = g[i * sub:(i + 1) * sub]
        for s in (1, 2, 4):
            x = x + jnp.where(row >= s, pltpu.roll(x, s, axis=0), 0.0)
        if run is not None:
            x = x + run
        run = x[sub - 1:sub]
        out.append(x)
    return jnp.concatenate(out, axis=0)


def _hgrn_body(q_ref, f_ref, i_ref, z_ref, og_ref, y_ref,
               st_ref, st0_ref, qrows_ref, frows_ref, krows_ref, *, rows):
    c = HGRN_CHUNK
    hd = HGRN_DIM
    heads = q_ref.shape[2] // hd
    chunks = rows // c
    t = pl.program_id(2)

    @pl.when(t == 0)
    def _():
        st_ref[...] = jnp.zeros_like(st_ref)

    st0_ref[...] = st_ref[...]
    ri = lax.broadcasted_iota(jnp.int32, (c, c), 0)
    ci = lax.broadcasted_iota(jnp.int32, (c, c), 1)
    causal = ri >= ci

    def lanes(h):
        return slice(h * hd, (h + 1) * hd)

    def finish(o, sl, h):
        ms = jnp.mean(o * o, axis=-1, keepdims=True)
        y = o * lax.rsqrt(ms + EPS) * og_ref[:, lanes(h)] * z_ref[0, sl, lanes(h)].astype(F32)
        y_ref[0, sl, lanes(h)] = y.astype(y_ref.dtype)

    units = [(h, n) for h in range(heads) for n in range(chunks)]
    worst = jnp.zeros((1, hd), F32)
    pre = {}
    for h, n in units:
        sl = slice(n * c, (n + 1) * c)
        q = q_ref[0, sl, lanes(h)].astype(F32)
        log2_f = f_ref[0, sl, lanes(h)].astype(F32)
        k = 1.0 - jnp.exp2(log2_f)
        gcum = _cumsum_rows(log2_f)
        rho = 0.5 * gcum[c - 1:c]
        worst = jnp.minimum(worst, rho)
        e_rho = jnp.exp2(rho)
        kh = k * jnp.exp2(rho - gcum)
        pre[h, n] = dict(qt=(q * jnp.exp2(gcum - rho)).astype(BF16), kh=kh.astype(BF16),
                         kd=(kh * e_rho).astype(BF16),
                         e_rho=e_rho, v=i_ref[0, sl, lanes(h)])
    for u in units:
        pre[u]["dst"] = lax.dot_general(pre[u]["v"], pre[u]["kd"], (((0,), (0,)), ((), ())),
                                        preferred_element_type=F32)
    for h in range(heads):
        st = st_ref[h]
        for n in range(chunks):
            d = pre[h, n]
            d["rhs"] = jnp.concatenate([d["kh"], (st * d["e_rho"]).astype(BF16)], axis=0)
            st = st * (d["e_rho"] * d["e_rho"]) + d["dst"]
        st_ref[h] = st
    for u in units:
        pre[u]["so"] = lax.dot_general(pre[u]["qt"], pre[u]["rhs"], (((1,), (1,)), ((), ())),
                                       preferred_element_type=F32)
    for u in units:
        scores = jnp.where(causal, pre[u]["so"][:, :c], 0.0).astype(BF16)
        pre[u]["o"] = pre[u]["so"][:, c:] + jnp.dot(scores, pre[u]["v"], preferred_element_type=F32)
    for h, n in units:
        finish(pre[h, n]["o"], slice(n * c, (n + 1) * c), h)

    @pl.when(jnp.min(worst) < -HGRN_MAX_HALF_DECAY)
    def _():
        lane = lax.broadcasted_iota(jnp.int32, (hd, c), 1)
        for h in range(heads):
            for n in range(chunks):
                sl = slice(n * c, (n + 1) * c)
                decay = jnp.exp2(f_ref[0, sl, lanes(h)].astype(F32))
                qrows_ref[...] = q_ref[0, sl, lanes(h)].astype(F32)
                frows_ref[...] = decay
                krows_ref[...] = 1.0 - decay
                vt = jnp.transpose(i_ref[0, sl, lanes(h)].astype(F32))

                def step(s, carry):
                    st, ot = carry
                    vcol = jnp.sum(jnp.where(lane == s, vt, 0.0), axis=-1, keepdims=True)
                    st = st * frows_ref[pl.ds(s, 1), :] + vcol * krows_ref[pl.ds(s, 1), :]
                    ocol = jnp.sum(st * qrows_ref[pl.ds(s, 1), :], axis=-1, keepdims=True)
                    ot = jnp.where(lane == s, ocol, ot)
                    return st, ot

                st, ot = lax.fori_loop(0, c, step, (st0_ref[h], jnp.zeros((hd, c), F32)))
                st0_ref[h] = st
                finish(jnp.transpose(ot), sl, h)
        st_ref[...] = st0_ref[...]


def _hgrn2(proj, o_gain, batch, seq):
    rows = HGRN_ROWS
    hd = HGRN_DIM
    heads = HGRN_HEADS_PER_STEP
    width = heads * hd
    projv = proj.reshape(batch, seq, IN_WIDTH)
    per_kind = HGRN_HEADS // heads

    def part(kind):
        return pl.BlockSpec((1, rows, width), lambda b, h, t: (b, t, kind * per_kind + h))

    def per_head():
        return pl.BlockSpec((1, width), lambda b, h, t: (0, h))

    y = pl.pallas_call(
        functools.partial(_hgrn_body, rows=rows),
        grid=(batch, per_kind, seq // rows),
        in_specs=[part(0), part(1), part(2), part(3), per_head()],
        out_specs=pl.BlockSpec((1, rows, width), lambda b, h, t: (b, t, h)),
        out_shape=jax.ShapeDtypeStruct((batch, seq, MIX_WIDTH), BF16),
        scratch_shapes=[pltpu.VMEM((heads, hd, hd), F32), pltpu.VMEM((heads, hd, hd), F32)]
        + [pltpu.VMEM((HGRN_CHUNK, hd), F32)] * 3,
        compiler_params=_cparams(("parallel", "parallel", "arbitrary")),
        name="hgrn2_recurrence",
    )(projv, projv, projv, projv, o_gain.reshape(1, -1).astype(F32))
    return y.reshape(batch * seq, MIX_WIDTH)


def _outproj_body(y_ref, w_ref, x_ref, o_ref):
    o_ref[...] = x_ref[...] + jnp.dot(y_ref[...], w_ref[...], preferred_element_type=F32)


def _outproj(y, w_bf16, x2d):
    t, d = x2d.shape
    tm = PROJ_TM
    return pl.pallas_call(
        _outproj_body,
        grid=(t // tm,),
        in_specs=[pl.BlockSpec((tm, y.shape[1]), lambda i: (i, 0)),
                  pl.BlockSpec(w_bf16.shape, lambda i: (0, 0)),
                  pl.BlockSpec((tm, d), lambda i: (i, 0))],
        out_specs=pl.BlockSpec((tm, d), lambda i: (i, 0)),
        out_shape=jax.ShapeDtypeStruct((t, d), F32),
        compiler_params=_cparams(("parallel",)),
        name="outproj_residual",
    )(y, w_bf16, x2d)


def kernel(x, ln_even, w_in_even, conv_w, q_gain, k_gain, w_out_even, rel_bias, ln_odd, w_in_odd,
           lower_bounds, o_gain, w_out_odd):
    batch, seq, d = x.shape
    assert d == D_MODEL and seq % (DILATED_PATTERNS[-1][1] * ATTN_BLOCK) == 0 and seq % PROJ_TM == 0
    depth = ln_even.shape[0] + ln_odd.shape[0]
    x2d = x.reshape(batch * seq, d).astype(F32)

    lbs = jnp.cumsum(jax.nn.softmax(lower_bounds.astype(F32), axis=0), axis=0)
    lbs = lbs - lbs[0:1]
    biases = [_band_bias(rel_bias, window, dilation) for window, dilation in DILATED_PATTERNS]
    ones = jnp.ones((1, IN_WIDTH), F32)
    q_tile, k_tile, _ = QKV_TILES

    for layer in range(depth):
        j = layer // 2
        if layer % 2 == 0:
            q_scale = jnp.tile(q_gain[j].astype(F32), ATTN_HEADS) * (ATTN_HEAD_DIM ** -0.5 * LOG2E)
            colscale = ones.at[0, q_tile * PROJ_TN:(q_tile + 1) * PROJ_TN].set(q_scale)
            colscale = colscale.at[0, k_tile * PROJ_TN:(k_tile + 1) * PROJ_TN].set(
                jnp.tile(k_gain[j].astype(F32), ATTN_HEADS))
            proj, qkv4, qkv16 = _norm_inproj(x2d, ln_even[j], w_in_even[j].astype(BF16), colscale, ones,
                                             EVEN_TILE_KINDS, False, batch, seq)
            prev = None
            for index, srcs in enumerate((proj, qkv4, qkv16)):
                prev = _attention_pattern(index, srcs, biases[index], prev, batch, seq)
            x2d = _outproj_even(proj, prev[0], conv_w[j], w_out_even[j].astype(BF16), x2d, seq)
        else:
            gate_cols = slice(ODD_TILE_KINDS.index("log2gate") * PROJ_TN,
                              (ODD_TILE_KINDS.index("log2gate") + ODD_TILE_KINDS.count("log2gate")) * PROJ_TN)
            log_lb = ones.at[0, gate_cols].set(jnp.log(lbs[j]))
            log1m_lb = ones.at[0, gate_cols].set(jnp.log1p(-lbs[j]))
            (proj,) = _norm_inproj(x2d, ln_odd[j], w_in_odd[j].astype(BF16), log_lb, log1m_lb,
                                   ODD_TILE_KINDS, j == 0, batch, seq)
            y = _hgrn2(proj, o_gain[j], batch, seq)
            x2d = _outproj(y, w_out_odd[j].astype(BF16), x2d)
    return x2d.reshape(batch, seq, d).astype(x.dtype)
```

```python
import functools
import math

import jax
import jax.numpy as jnp
import numpy as np
from jax import lax
from jax.experimental import pallas as pl
from jax.experimental.pallas import tpu as pltpu

F32 = jnp.float32
BF16 = jnp.bfloat16

D_MODEL = 1024
MIX_WIDTH = 2 * D_MODEL
CONV_WIDTH = MIX_WIDTH // 2
ATTN_HEAD_DIM = 64
ATTN_HEADS = 16
ATTN_WIDTH = ATTN_HEADS * ATTN_HEAD_DIM
ATTN_BLOCK = 128
DILATED_PATTERNS = ((128, 1), (512, 4), (2048, 16))
DILATION_STEP = 4
REL_BUCKETS = 32
REL_MAX_DISTANCE = 2048
HGRN_HEADS = 16
HGRN_DIM = 128
IN_WIDTH = 8192
QKV_TILES = (3, 4, 5)
EPS = 1e-6
MASKED = -1e30
LOG2E = math.log2(math.e)

V7X_LANES = 128
V7X_SUBLANES = 8
V7X_VMEM_LIMIT_BYTES = 56 * 1024 * 1024

PROJ_TM = 512
INPROJ_TM = 256
PROJ_TN = 1024
ATTN_ROWS = 512
ATTN_UNITS_PER_GROUP = 8
HGRN_ROWS = 512
HGRN_CHUNK = 128
HGRN_HEADS_PER_STEP = 4
HGRN_MAX_HALF_DECAY = 100.0

N_SLABS = ATTN_WIDTH // V7X_LANES


def _cparams(semantics):
    return pltpu.CompilerParams(dimension_semantics=semantics, vmem_limit_bytes=V7X_VMEM_LIMIT_BYTES)


def _silu(t):
    half = 0.5 * t
    return half + half * jnp.tanh(half)


EVEN_TILE_KINDS = ("plain", "plain", "plain", "headnorm", "headnorm", "scatter", "silu", "silu")
ODD_TILE_KINDS = ("silu", "silu", "log2gate", "log2gate", "plain", "plain", "silu", "silu")


def _log2_forget_gate(f_pre, lb_terms):
    log_f = jnp.minimum(f_pre, 0.0) - jnp.log(1.0 + jnp.exp(-jnp.abs(f_pre)))
    if lb_terms is not None:
        log_lb, log1m_lb = lb_terms
        b = log1m_lb + log_f
        log_f = jnp.maximum(log_lb, b) + jnp.log(1.0 + jnp.exp(-jnp.abs(log_lb - b)))
    return log_f * LOG2E


def _residue_major_permutation(rows, dilation):
    out = np.arange(rows)
    src = dilation * (out % (rows // dilation)) + out // (rows // dilation)
    return jnp.asarray(src[:, None] == np.arange(rows)[None, :], BF16)


def _inproj_body(x_ref, g_ref, w_ref, ca_ref, cb_ref, bd_ref, *rest, kinds, zero_lb):
    scatter = "scatter" in kinds
    if scatter:
        p4_ref, p16_ref, o_ref, o4_ref, o16_ref = rest
    else:
        (o_ref,) = rest
    x = x_ref[...]
    ms = jnp.mean(x * x, axis=-1, keepdims=True)
    hn = (x * lax.rsqrt(ms + EPS) * g_ref[...]).astype(BF16)
    tm = x.shape[0]
    tn = PROJ_TN

    def emit_residue_major(res, cols, part):
        res = res.astype(BF16)
        o_ref[:, cols] = res
        for perm_ref, ref in ((p4_ref, o4_ref), (p16_ref, o16_ref)):
            dil = ref.shape[0]
            moved = jnp.dot(perm_ref[...], res, preferred_element_type=F32).astype(ref.dtype)
            for r in range(dil):
                ref[r, :, part * tn:(part + 1) * tn] = moved[r * (tm // dil):(r + 1) * (tm // dil)]

    for jt, kind in enumerate(kinds):
        cols = slice(jt * tn, (jt + 1) * tn)
        acc = jnp.dot(hn, w_ref[:, cols], preferred_element_type=F32)
        if kind == "plain":
            o_ref[:, cols] = acc.astype(o_ref.dtype)
        elif kind == "silu":
            o_ref[:, cols] = _silu(acc).astype(o_ref.dtype)
        elif kind == "log2gate":
            lb_terms = None if zero_lb else (ca_ref[:, cols], cb_ref[:, cols])
            o_ref[:, cols] = _log2_forget_gate(acc, lb_terms).astype(o_ref.dtype)
        elif kind == "scatter":
            emit_residue_major(acc, cols, jt - QKV_TILES[0])
        else:
            width = bd_ref.shape[0]
            parts = []
            for s in range(tn // width):
                a = acc[:, s * width:(s + 1) * width]
                ms = jnp.dot((a * a).astype(BF16), bd_ref[...], preferred_element_type=F32)
                parts.append(a * lax.rsqrt(ms + EPS) * ca_ref[:, jt * tn + s * width:jt * tn + (s + 1) * width])
            emit_residue_major(jnp.concatenate(parts, axis=1), cols, jt - QKV_TILES[0])


def _norm_inproj(x2d, gain, w_bf16, col_a, col_b, kinds, zero_lb, batch, seq):
    t, d = x2d.shape
    n = w_bf16.shape[1]
    tm, tn = INPROJ_TM, PROJ_TN
    assert len(kinds) == n // tn
    width = 2 * V7X_LANES
    head_of_lane = np.arange(width) // ATTN_HEAD_DIM
    bd = jnp.asarray((head_of_lane[:, None] == head_of_lane[None, :]) / ATTN_HEAD_DIM, BF16)
    tiles_per_seq = seq // tm
    out_specs = [pl.BlockSpec((tm, n), lambda i: (i, 0))]
    out_shape = [jax.ShapeDtypeStruct((t, n), BF16)]
    perms = []
    if "scatter" in kinds:
        qkv_width = len(QKV_TILES) * tn
        for _, dil in DILATED_PATTERNS[1:]:
            out_specs.append(pl.BlockSpec((None, dil, tm // dil, qkv_width),
                                          lambda i: (i // tiles_per_seq, 0, i % tiles_per_seq, 0)))
            out_shape.append(jax.ShapeDtypeStruct((batch, dil, seq // dil, qkv_width), BF16))
            perms.append(_residue_major_permutation(tm, dil))
    resident = dict(pipeline_mode=pl.Buffered(1))
    return pl.pallas_call(
        functools.partial(_inproj_body, kinds=kinds, zero_lb=zero_lb),
        grid=(t // tm,),
        in_specs=[
            pl.BlockSpec((tm, d), lambda i: (i, 0)),
            pl.BlockSpec((1, d), lambda i: (0, 0)),
            pl.BlockSpec((d, n), lambda i: (0, 0), **resident),
            pl.BlockSpec((1, n), lambda i: (0, 0)),
            pl.BlockSpec((1, n), lambda i: (0, 0)),
            pl.BlockSpec((width, width), lambda i: (0, 0)),
        ] + [pl.BlockSpec((tm, tm), lambda i: (0, 0))] * len(perms),
        out_specs=out_specs,
        out_shape=out_shape,
        compiler_params=_cparams(("parallel",)),
        name="norm_inproj_even" if "scatter" in kinds else "norm_inproj_odd",
    )(x2d, gain.reshape(1, d).astype(F32), w_bf16, col_a, col_b, bd, *perms)


def _attn_body(*refs, has_prev, scatter, rows):
    refs = list(refs)
    qkv_ref, pqkv_ref, bias_ref = refs[:3]
    del refs[:3]
    if has_prev:
        po_ref, pl_ref = refs[:2]
        del refs[:2]
    if scatter:
        o_ref, l_ref, so_ref, sl_ref = refs
    else:
        (o_ref,) = refs
    blk = ATTN_BLOCK
    width = ATTN_WIDTH
    t = pl.program_id(2)

    def window(part, qb, c0):
        cols = slice(part * width + c0, part * width + c0 + V7X_LANES)
        if qb == 0:
            return jnp.concatenate([pqkv_ref[:, cols], qkv_ref[0:blk, cols]], axis=0)
        return qkv_ref[(qb - 1) * blk:(qb + 1) * blk, cols]

    lane = lax.broadcasted_iota(jnp.int32, (blk, V7X_LANES), 1)
    low = lane < ATTN_HEAD_DIM
    low_bf = jnp.where(low, 1.0, 0.0).astype(BF16)
    high_bf = jnp.where(low, 0.0, 1.0).astype(BF16)
    low2_bf = jnp.concatenate([low_bf, low_bf], axis=0)
    high2_bf = jnp.concatenate([high_bf, high_bf], axis=0)
    n_pairs = ATTN_HEADS // 2
    step = DILATION_STEP
    sub = blk // step

    def locate(idx):
        qb = idx // n_pairs
        hp = idx % n_pairs
        return qb, hp, qb * blk, hp * V7X_LANES

    def logits(idx):
        qb, hp, r0, c0 = locate(idx)
        q2 = qkv_ref[r0:r0 + blk, c0:c0 + V7X_LANES]
        kw = window(1, qb, c0)
        qs = jnp.concatenate([q2 * low_bf, q2 * high_bf], axis=0)
        return lax.dot_general(qs, kw, (((1,), (1,)), ((), ())), preferred_element_type=F32)

    def softmax(idx, s):
        qb, hp, _, _ = locate(idx)
        first = (t == 0).astype(jnp.int32) if qb == 0 else 0
        s = s + jnp.concatenate([bias_ref[first, 2 * hp], bias_ref[first, 2 * hp + 1]], axis=0)
        m = jnp.max(s, axis=-1, keepdims=True)
        p = jnp.exp2(s - m).astype(BF16)
        return p, jnp.where(low, m[0:blk], m[blk:2 * blk])

    def weighted(idx, p):
        qb, _, _, c0 = locate(idx)
        vw = window(2, qb, c0)
        rhs = jnp.concatenate([jnp.concatenate([vw * low2_bf, low2_bf], axis=1),
                               jnp.concatenate([vw * high2_bf, high2_bf], axis=1)], axis=0)
        lhs = jnp.concatenate([p[0:blk], p[blk:2 * blk]], axis=1)
        return jnp.dot(lhs, rhs, preferred_element_type=F32)

    def finish(idx, slot, pv, m2):
        qb, _, r0, c0 = locate(idx)
        den = pv[:, V7X_LANES:]
        o2 = pv[:, 0:V7X_LANES] / den
        if has_prev or scatter:
            lse2 = m2 + jnp.log(den) * LOG2E
        if has_prev:
            po = po_ref[pl.ds(r0, blk), pl.ds(c0, V7X_LANES)].astype(F32)
            plse = pl_ref[pl.ds(r0, blk), pl.ds(c0, V7X_LANES)]
            gap = plse - lse2
            e = jnp.exp2(-jnp.abs(gap))
            prev_larger = gap >= 0.0
            tot = 1.0 + e
            o2 = (jnp.where(prev_larger, po, o2) + e * jnp.where(prev_larger, o2, po)) / tot
            lse2 = jnp.maximum(plse, lse2) + jnp.log(tot) * LOG2E
        if not scatter:
            o_ref[pl.ds(r0, blk), pl.ds(c0, V7X_LANES)] = o2.astype(o_ref.dtype)
            return
        so_ref[slot] = o2
        sl_ref[slot] = lse2
        d0 = qb * sub
        for jm in range(step):
            o_ref[jm, pl.ds(d0, sub), pl.ds(c0, V7X_LANES)] = so_ref[
                slot, pl.ds(jm, sub, stride=step), :].astype(o_ref.dtype)
            l_ref[jm, pl.ds(d0, sub), pl.ds(c0, V7X_LANES)] = sl_ref[slot, pl.ds(jm, sub, stride=step), :]

    per_group = ATTN_UNITS_PER_GROUP
    n_units = (rows // blk) * n_pairs
    groups = [list(range(g, g + per_group)) for g in range(0, n_units, per_group)]
    pending = None
    for units in groups + [None]:
        pvs = None
        if pending is not None:
            prev_units, probs = pending
            pvs = [weighted(u, p) for u, (p, _) in zip(prev_units, probs)]
        scores = [logits(u) for u in units] if units is not None else None
        if pvs is not None:
            for slot, (u, pv, (_, m2)) in enumerate(zip(prev_units, pvs, probs)):
                finish(u, slot, pv, m2)
        pending = (units, [softmax(u, s) for u, s in zip(units, scores)]) if units is not None else None


def _attention_pattern(index, srcs, bias, prev, batch, seq):
    dilation = DILATED_PATTERNS[index][1]
    last = index == len(DILATED_PATTERNS) - 1
    length = seq // dilation
    rows = min(ATTN_ROWS, length)
    blk = ATTN_BLOCK
    width = ATTN_WIDTH
    per_step = rows // blk

    qkv_width = len(QKV_TILES) * width
    if index == 0:
        src = srcs.reshape(batch, seq, IN_WIDTH)
        assert QKV_TILES[0] * width == qkv_width
        cur = pl.BlockSpec((None, rows, qkv_width), lambda b, r, t: (b, t, 1))
        prv = pl.BlockSpec((None, blk, qkv_width), lambda b, r, t: (b, jnp.maximum(t * per_step - 1, 0), 1))
    else:
        src = srcs
        cur = pl.BlockSpec((None, None, rows, qkv_width), lambda b, r, t: (b, r, t, 0))
        prv = pl.BlockSpec((None, None, blk, qkv_width),
                           lambda b, r, t: (b, r, jnp.maximum(t * per_step - 1, 0), 0))

    res_blk = pl.BlockSpec((None, None, rows, width), lambda b, r, t: (b, r, t, 0))
    in_specs = [cur, prv, pl.BlockSpec(bias.shape, lambda b, r, t: (0, 0, 0, 0))]
    args = [src, src, bias]
    if prev is not None:
        in_specs += [res_blk, res_blk]
        args += [prev[0].reshape(batch, dilation, length, width), prev[1].reshape(batch, dilation, length, width)]
    scratch = []
    if last:
        out_specs = [res_blk]
        out_shape = [jax.ShapeDtypeStruct((batch, dilation, length, width), BF16)]
    else:
        step = DILATION_STEP
        shape = (batch, step, dilation, length // step, width)
        out_blk = pl.BlockSpec((None, step, None, rows // step, width), lambda b, r, t: (b, 0, r, t, 0))
        out_specs = [out_blk, out_blk]
        out_shape = [jax.ShapeDtypeStruct(shape, BF16), jax.ShapeDtypeStruct(shape, F32)]
        scratch += [pltpu.VMEM((ATTN_UNITS_PER_GROUP, blk, V7X_LANES), F32)] * 2
    return pl.pallas_call(
        functools.partial(_attn_body, has_prev=prev is not None, scatter=not last, rows=rows),
        grid=(batch, dilation, length // rows),
        in_specs=in_specs,
        out_specs=out_specs,
        out_shape=out_shape,
        scratch_shapes=scratch,
        compiler_params=_cparams(("parallel", "parallel", "arbitrary")),
        name=f"dilated_attention_d{dilation}",
    )(*args)


def _t5_bucket(distance):
    max_exact = REL_BUCKETS // 2
    scaled = jnp.log(jnp.maximum(distance, max_exact).astype(F32) / max_exact) / math.log(
        REL_MAX_DISTANCE / max_exact)
    large = jnp.minimum(max_exact + (scaled * (REL_BUCKETS - max_exact)).astype(jnp.int32), REL_BUCKETS - 1)
    return jnp.where(distance < max_exact, distance, large)


def _band_bias(rel_bias, window, dilation):
    blk = ATTN_BLOCK
    n_back = window // dilation
    qi = jnp.arange(blk)[:, None]
    kj = jnp.arange(2 * blk)[None, :]
    delta = blk + qi - kj
    band = (delta >= 0) & (delta <= n_back)
    onehot = jax.nn.one_hot(_t5_bucket(jnp.maximum(delta, 0) * dilation), REL_BUCKETS, dtype=F32)
    bias = jnp.einsum("qkb,bh->hqk", onehot, rel_bias.astype(F32) * LOG2E, precision=lax.Precision.HIGHEST)
    bias = jnp.where(band[None], bias, MASKED)
    return jnp.stack([bias, jnp.where(kj[None] >= blk, bias, MASKED)])


def _outproj_even_body(gb_ref, gc_ref, xa_ref, z_ref, oa_ref, gch_ref, xah_ref, cw_ref, w_ref, x_ref,
                       o_ref, slab_ref, *, tiles_per_seq):
    i = pl.program_id(0)
    tm = gc_ref.shape[0]
    dil = oa_ref.shape[0]
    for r in range(dil):
        for s in range(N_SLABS):
            slab_ref[s, pl.ds(r, tm // dil, stride=dil), :] = oa_ref[
                r, :, s * V7X_LANES:(s + 1) * V7X_LANES].astype(F32)
    attn = jnp.concatenate([slab_ref[s] for s in range(N_SLABS)], axis=1)

    u = gc_ref[...].astype(F32) * xa_ref[...].astype(F32)
    halo = gch_ref[...].astype(F32) * xah_ref[...].astype(F32)
    halo = jnp.where((i % tiles_per_seq) == 0, 0.0, halo)
    row = lax.broadcasted_iota(jnp.int32, u.shape, 0)
    h6 = halo[V7X_SUBLANES - 2:V7X_SUBLANES - 1]
    h7 = halo[V7X_SUBLANES - 1:V7X_SUBLANES]
    u1 = jnp.where(row >= 1, pltpu.roll(u, 1, axis=0), h7)
    u2 = jnp.where(row >= 2, pltpu.roll(u, 2, axis=0), jnp.where(row == 0, h6, h7))
    cw = cw_ref[...]
    conv = cw[0:1] * u2 + cw[1:2] * u1 + cw[2:3] * u
    gate = z_ref[...].astype(F32)
    half = CONV_WIDTH
    ya = (gb_ref[...].astype(F32) * conv * gate[:, :half]).astype(BF16)
    yb = (attn * gate[:, half:]).astype(BF16)
    acc = jnp.dot(ya, w_ref[0:half, :], preferred_element_type=F32)
    acc += jnp.dot(yb, w_ref[half:, :], preferred_element_type=F32)
    o_ref[...] = x_ref[...] + acc


def _outproj_even(proj, o_attn, conv_w, w_bf16, x2d, seq):
    t, d = x2d.shape
    tm = PROJ_TM
    w = CONV_WIDTH
    halo_blocks = tm // V7X_SUBLANES
    tiles_per_seq = seq // tm
    dil = o_attn.shape[1]

    def colblk(c, width=w):
        return pl.BlockSpec((tm, width), lambda i: (i, c))

    def halo(c):
        return pl.BlockSpec((V7X_SUBLANES, w), lambda i: (jnp.maximum(i * halo_blocks - 1, 0), c))

    return pl.pallas_call(
        functools.partial(_outproj_even_body, tiles_per_seq=tiles_per_seq),
        grid=(t // tm,),
        in_specs=[colblk(0), colblk(1), colblk(2), colblk(3, MIX_WIDTH),
                  pl.BlockSpec((None, dil, tm // dil, w), lambda i: (i // tiles_per_seq, 0, i % tiles_per_seq, 0)),
                  halo(1), halo(2),
                  pl.BlockSpec(conv_w.shape, lambda i: (0, 0)),
                  pl.BlockSpec(w_bf16.shape, lambda i: (0, 0)),
                  pl.BlockSpec((tm, d), lambda i: (i, 0))],
        out_specs=pl.BlockSpec((tm, d), lambda i: (i, 0)),
        out_shape=jax.ShapeDtypeStruct((t, d), F32),
        scratch_shapes=[pltpu.VMEM((N_SLABS, tm, V7X_LANES), F32)],
        compiler_params=_cparams(("parallel",)),
        name="conv_gate_outproj",
    )(proj, proj, proj, proj, o_attn, proj, proj, conv_w.astype(F32), w_bf16, x2d)


def _cumsum_rows(g):
    sub = V7X_SUBLANES
    row = lax.broadcasted_iota(jnp.int32, (sub, g.shape[1]), 0)
    out = []
    run = None
    for i in range(g.shape[0] // sub):
        x = g[i * sub:(i + 1) * sub]
        for s in (1, 2, 4):
            x = x + jnp.where(row >= s, pltpu.roll(x, s, axis=0), 0.0)
        if run is not None:
            x = x + run
        run = x[sub - 1:sub]
        out.append(x)
    return jnp.concatenate(out, axis=0)


def _hgrn_body(q_ref, f_ref, i_ref, z_ref, og_ref, y_ref,
               st_ref, st0_ref, qrows_ref, frows_ref, krows_ref, *, rows):
    c = HGRN_CHUNK
    hd = HGRN_DIM
    heads = q_ref.shape[2] // hd
    chunks = rows // c
    t = pl.program_id(2)

    @pl.when(t == 0)
    def _():
        st_ref[...] = jnp.zeros_like(st_ref)

    st0_ref[...] = st_ref[...]
    ri = lax.broadcasted_iota(jnp.int32, (c, c), 0)
    ci = lax.broadcasted_iota(jnp.int32, (c, c), 1)
    causal = ri >= ci

    def lanes(h):
        return slice(h * hd, (h + 1) * hd)

    def finish(o, sl, h):
        ms = jnp.mean(o * o, axis=-1, keepdims=True)
        y = o * lax.rsqrt(ms + EPS) * og_ref[:, lanes(h)] * z_ref[0, sl, lanes(h)].astype(F32)
        y_ref[0, sl, lanes(h)] = y.astype(y_ref.dtype)

    units = [(h, n) for h in range(heads) for n in range(chunks)]
    worst = jnp.zeros((1, hd), F32)
    pre = {}
    for h, n in units:
        sl = slice(n * c, (n + 1) * c)
        q = q_ref[0, sl, lanes(h)].astype(F32)
        log2_f = f_ref[0, sl, lanes(h)].astype(F32)
        k = 1.0 - jnp.exp2(log2_f)
        gcum = _cumsum_rows(log2_f)
        rho = 0.5 * gcum[c - 1:c]
        worst = jnp.minimum(worst, rho)
        e_rho = jnp.exp2(rho)
        kh = k * jnp.exp2(rho - gcum)
        pre[h, n] = dict(qt=(q * jnp.exp2(gcum - rho)).astype(BF16), kh=kh.astype(BF16),
                         kd=(kh * e_rho).astype(BF16),
                         e_rho=e_rho, v=i_ref[0, sl, lanes(h)])
    for u in units:
        pre[u]["dst"] = lax.dot_general(pre[u]["v"], pre[u]["kd"], (((0,), (0,)), ((), ())),
                                        preferred_element_type=F32)
    for h in range(heads):
        st = st_ref[h]
        for n in range(chunks):
            d = pre[h, n]
            d["rhs"] = jnp.concatenate([d["kh"], (st * d["e_rho"]).astype(BF16)], axis=0)
            st = st * (d["e_rho"] * d["e_rho"]) + d["dst"]
        st_ref[h] = st
    for u in units:
        pre[u]["so"] = lax.dot_general(pre[u]["qt"], pre[u]["rhs"], (((1,), (1,)), ((), ())),
                                       preferred_element_type=F32)
    for u in units:
        scores = jnp.where(causal, pre[u]["so"][:, :c], 0.0).astype(BF16)
        pre[u]["o"] = pre[u]["so"][:, c:] + jnp.dot(scores, pre[u]["v"], preferred_element_type=F32)
    for h, n in units:
        finish(pre[h, n]["o"], slice(n * c, (n + 1) * c), h)

    @pl.when(jnp.min(worst) < -HGRN_MAX_HALF_DECAY)
    def _():
        lane = lax.broadcasted_iota(jnp.int32, (hd, c), 1)
        for h in range(heads):
            for n in range(chunks):
                sl = slice(n * c, (n + 1) * c)
                decay = jnp.exp2(f_ref[0, sl, lanes(h)].astype(F32))
                qrows_ref[...] = q_ref[0, sl, lanes(h)].astype(F32)
                frows_ref[...] = decay
                krows_ref[...] = 1.0 - decay
                vt = jnp.transpose(i_ref[0, sl, lanes(h)].astype(F32))

                def step(s, carry):
                    st, ot = carry
                    vcol = jnp.sum(jnp.where(lane == s, vt, 0.0), axis=-1, keepdims=True)
                    st = st * frows_ref[pl.ds(s, 1), :] + vcol * krows_ref[pl.ds(s, 1), :]
                    ocol = jnp.sum(st * qrows_ref[pl.ds(s, 1), :], axis=-1, keepdims=True)
                    ot = jnp.where(lane == s, ocol, ot)
                    return st, ot

                st, ot = lax.fori_loop(0, c, step, (st0_ref[h], jnp.zeros((hd, c), F32)))
                st0_ref[h] = st
                finish(jnp.transpose(ot), sl, h)
        st_ref[...] = st0_ref[...]


def _hgrn2(proj, o_gain, batch, seq):
    rows = HGRN_ROWS
    hd = HGRN_DIM
    heads = HGRN_HEADS_PER_STEP
    width = heads * hd
    projv = proj.reshape(batch, seq, IN_WIDTH)
    per_kind = HGRN_HEADS // heads

    def part(kind):
        return pl.BlockSpec((1, rows, width), lambda b, h, t: (b, t, kind * per_kind + h))

    def per_head():
        return pl.BlockSpec((1, width), lambda b, h, t: (0, h))

    y = pl.pallas_call(
        functools.partial(_hgrn_body, rows=rows),
        grid=(batch, per_kind, seq // rows),
        in_specs=[part(0), part(1), part(2), part(3), per_head()],
        out_specs=pl.BlockSpec((1, rows, width), lambda b, h, t: (b, t, h)),
        out_shape=jax.ShapeDtypeStruct((batch, seq, MIX_WIDTH), BF16),
        scratch_shapes=[pltpu.VMEM((heads, hd, hd), F32), pltpu.VMEM((heads, hd, hd), F32)]
        + [pltpu.VMEM((HGRN_CHUNK, hd), F32)] * 3,
        compiler_params=_cparams(("parallel", "parallel", "arbitrary")),
        name="hgrn2_recurrence",
    )(projv, projv, projv, projv, o_gain.reshape(1, -1).astype(F32))
    return y.reshape(batch * seq, MIX_WIDTH)


def _outproj_body(y_ref, w_ref, x_ref, o_ref):
    o_ref[...] = x_ref[...] + jnp.dot(y_ref[...], w_ref[...], preferred_element_type=F32)


def _outproj(y, w_bf16, x2d):
    t, d = x2d.shape
    tm = PROJ_TM
    return pl.pallas_call(
        _outproj_body,
        grid=(t // tm,),
        in_specs=[pl.BlockSpec((tm, y.shape[1]), lambda i: (i, 0)),
                  pl.BlockSpec(w_bf16.shape, lambda i: (0, 0)),
                  pl.BlockSpec((tm, d), lambda i: (i, 0))],
        out_specs=pl.BlockSpec((tm, d), lambda i: (i, 0)),
        out_shape=jax.ShapeDtypeStruct((t, d), F32),
        compiler_params=_cparams(("parallel",)),
        name="outproj_residual",
    )(y, w_bf16, x2d)


def kernel(x, ln_even, w_in_even, conv_w, q_gain, k_gain, w_out_even, rel_bias, ln_odd, w_in_odd,
           lower_bounds, o_gain, w_out_odd):
    batch, seq, d = x.shape
    assert d == D_MODEL and seq % (DILATED_PATTERNS[-1][1] * ATTN_BLOCK) == 0 and seq % PROJ_TM == 0
    depth = ln_even.shape[0] + ln_odd.shape[0]
    x2d = x.reshape(batch * seq, d).astype(F32)

    lbs = jnp.cumsum(jax.nn.softmax(lower_bounds.astype(F32), axis=0), axis=0)
    lbs = lbs - lbs[0:1]
    biases = [_band_bias(rel_bias, window, dilation) for window, dilation in DILATED_PATTERNS]
    ones = jnp.ones((1, IN_WIDTH), F32)
    q_tile, k_tile, _ = QKV_TILES

    for layer in range(depth):
        j = layer // 2
        if layer % 2 == 0:
            q_scale = jnp.tile(q_gain[j].astype(F32), ATTN_HEADS) * (ATTN_HEAD_DIM ** -0.5 * LOG2E)
            colscale = ones.at[0, q_tile * PROJ_TN:(q_tile + 1) * PROJ_TN].set(q_scale)
            colscale = colscale.at[0, k_tile * PROJ_TN:(k_tile + 1) * PROJ_TN].set(
                jnp.tile(k_gain[j].astype(F32), ATTN_HEADS))
            proj, qkv4, qkv16 = _norm_inproj(x2d, ln_even[j], w_in_even[j].astype(BF16), colscale, ones,
                                             EVEN_TILE_KINDS, False, batch, seq)
            prev = None
            for index, srcs in enumerate((proj, qkv4, qkv16)):
                prev = _attention_pattern(index, srcs, biases[index], prev, batch, seq)
            x2d = _outproj_even(proj, prev[0], conv_w[j], w_out_even[j].astype(BF16), x2d, seq)
        else:
            gate_cols = slice(ODD_TILE_KINDS.index("log2gate") * PROJ_TN,
                              (ODD_TILE_KINDS.index("log2gate") + ODD_TILE_KINDS.count("log2gate")) * PROJ_TN)
            log_lb = ones.at[0, gate_cols].set(jnp.log(lbs[j]))
            log1m_lb = ones.at[0, gate_cols].set(jnp.log1p(-lbs[j]))
            (proj,) = _norm_inproj(x2d, ln_odd[j], w_in_odd[j].astype(BF16), log_lb, log1m_lb,
                                   ODD_TILE_KINDS, j == 0, batch, seq)
            y = _hgrn2(proj, o_gain[j], batch, seq)
            x2d = _outproj(y, w_out_odd[j].astype(BF16), x2d)
    return x2d.reshape(batch, seq, d).astype(x.dtype)
```

```python
import functools
import math

import jax
import jax.numpy as jnp
import numpy as np
from jax import lax
from jax.experimental import pallas as pl
from jax.experimental.pallas import tpu as pltpu

F32 = jnp.float32
BF16 = jnp.bfloat16

D_MODEL = 1024
MIX_WIDTH = 2 * D_MODEL
CONV_WIDTH = MIX_WIDTH // 2
ATTN_HEAD_DIM = 64
ATTN_HEADS = 16
ATTN_WIDTH = ATTN_HEADS * ATTN_HEAD_DIM
ATTN_BLOCK = 128
DILATED_PATTERNS = ((128, 1), (512, 4), (2048, 16))
DILATION_STEP = 4
REL_BUCKETS = 32
REL_MAX_DISTANCE = 2048
HGRN_HEADS = 16
HGRN_DIM = 128
IN_WIDTH = 8192
QKV_TILES = (3, 4, 5)
EPS = 1e-6
MASKED = -1e30
LOG2E = math.log2(math.e)

V7X_LANES = 128
V7X_SUBLANES = 8
V7X_VMEM_LIMIT_BYTES = 56 * 1024 * 1024

PROJ_TM = 512
INPROJ_TM_EVEN = 256
INPROJ_TM_ODD = 512
PROJ_TN = 1024
ATTN_ROWS = 512
ATTN_UNITS_PER_GROUP = 8
HGRN_ROWS = 512
HGRN_CHUNK = 128
HGRN_HEADS_PER_STEP = 4
HGRN_MAX_HALF_DECAY = 100.0

N_SLABS = ATTN_WIDTH // V7X_LANES


def _cparams(semantics):
    return pltpu.CompilerParams(dimension_semantics=semantics, vmem_limit_bytes=V7X_VMEM_LIMIT_BYTES)


def _silu(t):
    half = 0.5 * t
    return half + half * jnp.tanh(half)


EVEN_TILE_KINDS = ("plain", "plain", "plain", "headnorm", "headnorm", "scatter", "silu", "silu")
ODD_TILE_KINDS = ("silu", "silu", "log2gate", "log2gate", "plain", "plain", "silu", "silu")


def _log2_forget_gate(f_pre, lb_terms):
    if lb_terms is None:
        log_f = jnp.minimum(f_pre, 0.0) - jnp.log(1.0 + jnp.exp(-jnp.abs(f_pre)))
    else:
        mid, half_range = lb_terms
        log_f = jnp.log(mid + half_range * jnp.tanh(0.5 * f_pre))
    return log_f * LOG2E


def _residue_major_permutation(rows, dilation):
    out = np.arange(rows)
    src = dilation * (out % (rows // dilation)) + out // (rows // dilation)
    return jnp.asarray(src[:, None] == np.arange(rows)[None, :], BF16)


def _inproj_body(x_ref, g_ref, w_ref, ca_ref, cb_ref, bd_ref, *rest, kinds, zero_lb):
    scatter = "scatter" in kinds
    if scatter:
        p4_ref, p16_ref, o_ref, o4_ref, o16_ref = rest
    else:
        (o_ref,) = rest
    x = x_ref[...]
    ms = jnp.mean(x * x, axis=-1, keepdims=True)
    hn = (x * lax.rsqrt(ms + EPS) * g_ref[...]).astype(BF16)
    tm = x.shape[0]
    tn = PROJ_TN

    def emit_residue_major(res, cols, part):
        res = res.astype(BF16)
        o_ref[:, cols] = res
        for perm_ref, ref in ((p4_ref, o4_ref), (p16_ref, o16_ref)):
            dil = ref.shape[0]
            moved = jnp.dot(perm_ref[...], res, preferred_element_type=F32).astype(ref.dtype)
            for r in range(dil):
                ref[r, :, part * tn:(part + 1) * tn] = moved[r * (tm // dil):(r + 1) * (tm // dil)]

    for jt, kind in enumerate(kinds):
        cols = slice(jt * tn, (jt + 1) * tn)
        acc = jnp.dot(hn, w_ref[:, cols], preferred_element_type=F32)
        if kind == "plain":
            o_ref[:, cols] = acc.astype(o_ref.dtype)
        elif kind == "silu":
            o_ref[:, cols] = _silu(acc).astype(o_ref.dtype)
        elif kind == "log2gate":
            lb_terms = None if zero_lb else (ca_ref[:, cols], cb_ref[:, cols])
            o_ref[:, cols] = _log2_forget_gate(acc, lb_terms).astype(o_ref.dtype)
        elif kind == "scatter":
            emit_residue_major(acc, cols, jt - QKV_TILES[0])
        else:
            width = bd_ref.shape[0]
            parts = []
            for s in range(tn // width):
                a = acc[:, s * width:(s + 1) * width]
                ms = jnp.dot((a * a).astype(BF16), bd_ref[...], preferred_element_type=F32)
                parts.append(a * lax.rsqrt(ms + EPS) * ca_ref[:, jt * tn + s * width:jt * tn + (s + 1) * width])
            emit_residue_major(jnp.concatenate(parts, axis=1), cols, jt - QKV_TILES[0])


def _norm_inproj(x2d, gain, w_bf16, col_a, col_b, kinds, zero_lb, batch, seq):
    t, d = x2d.shape
    n = w_bf16.shape[1]
    tm, tn = (INPROJ_TM_EVEN if "scatter" in kinds else INPROJ_TM_ODD), PROJ_TN
    assert len(kinds) == n // tn
    width = 2 * V7X_LANES
    head_of_lane = np.arange(width) // ATTN_HEAD_DIM
    bd = jnp.asarray((head_of_lane[:, None] == head_of_lane[None, :]) / ATTN_HEAD_DIM, BF16)
    tiles_per_seq = seq // tm
    out_specs = [pl.BlockSpec((tm, n), lambda i: (i, 0))]
    out_shape = [jax.ShapeDtypeStruct((t, n), BF16)]
    perms = []
    if "scatter" in kinds:
        qkv_width = len(QKV_TILES) * tn
        for _, dil in DILATED_PATTERNS[1:]:
            out_specs.append(pl.BlockSpec((None, dil, tm // dil, qkv_width),
                                          lambda i: (i // tiles_per_seq, 0, i % tiles_per_seq, 0)))
            out_shape.append(jax.ShapeDtypeStruct((batch, dil, seq // dil, qkv_width), BF16))
            perms.append(_residue_major_permutation(tm, dil))
    resident = dict(pipeline_mode=pl.Buffered(1))
    return pl.pallas_call(
        functools.partial(_inproj_body, kinds=kinds, zero_lb=zero_lb),
        grid=(t // tm,),
        in_specs=[
            pl.BlockSpec((tm, d), lambda i: (i, 0)),
            pl.BlockSpec((1, d), lambda i: (0, 0)),
            pl.BlockSpec((d, n), lambda i: (0, 0), **resident),
            pl.BlockSpec((1, n), lambda i: (0, 0)),
            pl.BlockSpec((1, n), lambda i: (0, 0)),
            pl.BlockSpec((width, width), lambda i: (0, 0)),
        ] + [pl.BlockSpec((tm, tm), lambda i: (0, 0))] * len(perms),
        out_specs=out_specs,
        out_shape=out_shape,
        compiler_params=_cparams(("parallel",)),
        name="norm_inproj_even" if "scatter" in kinds else "norm_inproj_odd",
    )(x2d, gain.reshape(1, d).astype(F32), w_bf16, col_a, col_b, bd, *perms)


def _attn_body(*refs, has_prev, scatter, rows):
    refs = list(refs)
    qkv_ref, pqkv_ref, bias_ref = refs[:3]
    del refs[:3]
    if has_prev:
        po_ref, pl_ref = refs[:2]
        del refs[:2]
    if scatter:
        o_ref, l_ref, so_ref, sl_ref = refs
    else:
        (o_ref,) = refs
    blk = ATTN_BLOCK
    width = ATTN_WIDTH
    t = pl.program_id(2)

    def window(part, qb, c0):
        cols = slice(part * width + c0, part * width + c0 + V7X_LANES)
        if qb == 0:
            return jnp.concatenate([pqkv_ref[:, cols], qkv_ref[0:blk, cols]], axis=0)
        return qkv_ref[(qb - 1) * blk:(qb + 1) * blk, cols]

    lane = lax.broadcasted_iota(jnp.int32, (blk, V7X_LANES), 1)
    low = lane < ATTN_HEAD_DIM
    low_bf = jnp.where(low, 1.0, 0.0).astype(BF16)
    high_bf = jnp.where(low, 0.0, 1.0).astype(BF16)
    low2_bf = jnp.concatenate([low_bf, low_bf], axis=0)
    high2_bf = jnp.concatenate([high_bf, high_bf], axis=0)
    n_pairs = ATTN_HEADS // 2
    step = DILATION_STEP
    sub = blk // step

    def locate(idx):
        qb = idx // n_pairs
        hp = idx % n_pairs
        return qb, hp, qb * blk, hp * V7X_LANES

    def logits(idx):
        qb, hp, r0, c0 = locate(idx)
        q2 = qkv_ref[r0:r0 + blk, c0:c0 + V7X_LANES]
        kw = window(1, qb, c0)
        qs = jnp.concatenate([q2 * low_bf, q2 * high_bf], axis=0)
        return lax.dot_general(qs, kw, (((1,), (1,)), ((), ())), preferred_element_type=F32)

    def softmax(idx, s):
        qb, hp, _, _ = locate(idx)
        first = (t == 0).astype(jnp.int32) if qb == 0 else 0
        s = s + jnp.concatenate([bias_ref[first, 2 * hp], bias_ref[first, 2 * hp + 1]], axis=0)
        m = jnp.max(s, axis=-1, keepdims=True)
        p = jnp.exp2(s - m).astype(BF16)
        return p, jnp.where(low, m[0:blk], m[blk:2 * blk])

    def weighted(idx, p):
        qb, _, _, c0 = locate(idx)
        vw = window(2, qb, c0)
        rhs = jnp.concatenate([jnp.concatenate([vw * low2_bf, low2_bf], axis=1),
                               jnp.concatenate([vw * high2_bf, high2_bf], axis=1)], axis=0)
        lhs = jnp.concatenate([p[0:blk], p[blk:2 * blk]], axis=1)
        return jnp.dot(lhs, rhs, preferred_element_type=F32)

    def finish(idx, slot, pv, m2):
        qb, _, r0, c0 = locate(idx)
        den = pv[:, V7X_LANES:]
        o2 = pv[:, 0:V7X_LANES] / den
        if has_prev or scatter:
            lse2 = m2 + jnp.log(den) * LOG2E
        if has_prev:
            po = po_ref[pl.ds(r0, blk), pl.ds(c0, V7X_LANES)].astype(F32)
            plse = pl_ref[pl.ds(r0, blk), pl.ds(c0, V7X_LANES)]
            gap = plse - lse2
            e = jnp.exp2(-jnp.abs(gap))
            prev_larger = gap >= 0.0
            tot = 1.0 + e
            o2 = (jnp.where(prev_larger, po, o2) + e * jnp.where(prev_larger, o2, po)) / tot
            lse2 = jnp.maximum(plse, lse2) + jnp.log(tot) * LOG2E
        if not scatter:
            o_ref[pl.ds(r0, blk), pl.ds(c0, V7X_LANES)] = o2.astype(o_ref.dtype)
            return
        so_ref[slot] = o2
        sl_ref[slot] = lse2
        d0 = qb * sub
        for jm in range(step):
            o_ref[jm, pl.ds(d0, sub), pl.ds(c0, V7X_LANES)] = so_ref[
                slot, pl.ds(jm, sub, stride=step), :].astype(o_ref.dtype)
            l_ref[jm, pl.ds(d0, sub), pl.ds(c0, V7X_LANES)] = sl_ref[slot, pl.ds(jm, sub, stride=step), :]

    per_group = ATTN_UNITS_PER_GROUP
    n_units = (rows // blk) * n_pairs
    groups = [list(range(g, g + per_group)) for g in range(0, n_units, per_group)]
    pending = None
    for units in groups + [None]:
        pvs = None
        if pending is not None:
            prev_units, probs = pending
            pvs = [weighted(u, p) for u, (p, _) in zip(prev_units, probs)]
        scores = [logits(u) for u in units] if units is not None else None
        if pvs is not None:
            for slot, (u, pv, (_, m2)) in enumerate(zip(prev_units, pvs, probs)):
                finish(u, slot, pv, m2)
        pending = (units, [softmax(u, s) for u, s in zip(units, scores)]) if units is not None else None


def _attention_pattern(index, srcs, bias, prev, batch, seq):
    dilation = DILATED_PATTERNS[index][1]
    last = index == len(DILATED_PATTERNS) - 1
    length = seq // dilation
    rows = min(ATTN_ROWS, length)
    blk = ATTN_BLOCK
    width = ATTN_WIDTH
    per_step = rows // blk

    qkv_width = len(QKV_TILES) * width
    if index == 0:
        src = srcs.reshape(batch, seq, IN_WIDTH)
        assert QKV_TILES[0] * width == qkv_width
        cur = pl.BlockSpec((None, rows, qkv_width), lambda b, r, t: (b, t, 1))
        prv = pl.BlockSpec((None, blk, qkv_width), lambda b, r, t: (b, jnp.maximum(t * per_step - 1, 0), 1))
    else:
        src = srcs
        cur = pl.BlockSpec((None, None, rows, qkv_width), lambda b, r, t: (b, r, t, 0))
        prv = pl.BlockSpec((None, None, blk, qkv_width),
                           lambda b, r, t: (b, r, jnp.maximum(t * per_step - 1, 0), 0))

    res_blk = pl.BlockSpec((None, None, rows, width), lambda b, r, t: (b, r, t, 0))
    in_specs = [cur, prv, pl.BlockSpec(bias.shape, lambda b, r, t: (0, 0, 0, 0))]
    args = [src, src, bias]
    if prev is not None:
        in_specs += [res_blk, res_blk]
        args += [prev[0].reshape(batch, dilation, length, width), prev[1].reshape(batch, dilation, length, width)]
    scratch = []
    if last:
        out_specs = [res_blk]
        out_shape = [jax.ShapeDtypeStruct((batch, dilation, length, width), BF16)]
    else:
        step = DILATION_STEP
        shape = (batch, step, dilation, length // step, width)
        out_blk = pl.BlockSpec((None, step, None, rows // step, width), lambda b, r, t: (b, 0, r, t, 0))
        out_specs = [out_blk, out_blk]
        out_shape = [jax.ShapeDtypeStruct(shape, BF16), jax.ShapeDtypeStruct(shape, F32)]
        scratch += [pltpu.VMEM((ATTN_UNITS_PER_GROUP, blk, V7X_LANES), F32)] * 2
    return pl.pallas_call(
        functools.partial(_attn_body, has_prev=prev is not None, scatter=not last, rows=rows),
        grid=(batch, dilation, length // rows),
        in_specs=in_specs,
        out_specs=out_specs,
        out_shape=out_shape,
        scratch_shapes=scratch,
        compiler_params=_cparams(("parallel", "parallel", "arbitrary")),
        name=f"dilated_attention_d{dilation}",
    )(*args)


def _t5_bucket(distance):
    max_exact = REL_BUCKETS // 2
    scaled = jnp.log(jnp.maximum(distance, max_exact).astype(F32) / max_exact) / math.log(
        REL_MAX_DISTANCE / max_exact)
    large = jnp.minimum(max_exact + (scaled * (REL_BUCKETS - max_exact)).astype(jnp.int32), REL_BUCKETS - 1)
    return jnp.where(distance < max_exact, distance, large)


def _band_bias(rel_bias, window, dilation):
    blk = ATTN_BLOCK
    n_back = window // dilation
    qi = jnp.arange(blk)[:, None]
    kj = jnp.arange(2 * blk)[None, :]
    delta = blk + qi - kj
    band = (delta >= 0) & (delta <= n_back)
    onehot = jax.nn.one_hot(_t5_bucket(jnp.maximum(delta, 0) * dilation), REL_BUCKETS, dtype=F32)
    bias = jnp.einsum("qkb,bh->hqk", onehot, rel_bias.astype(F32) * LOG2E, precision=lax.Precision.HIGHEST)
    bias = jnp.where(band[None], bias, MASKED)
    return jnp.stack([bias, jnp.where(kj[None] >= blk, bias, MASKED)])


def _outproj_even_body(gb_ref, gc_ref, xa_ref, z_ref, oa_ref, gch_ref, xah_ref, cw_ref, w_ref, x_ref,
                       o_ref, slab_ref, *, tiles_per_seq):
    i = pl.program_id(0)
    tm = gc_ref.shape[0]
    dil = oa_ref.shape[0]
    for r in range(dil):
        for s in range(N_SLABS):
            slab_ref[s, pl.ds(r, tm // dil, stride=dil), :] = oa_ref[
                r, :, s * V7X_LANES:(s + 1) * V7X_LANES].astype(F32)
    attn = jnp.concatenate([slab_ref[s] for s in range(N_SLABS)], axis=1)

    u = gc_ref[...].astype(F32) * xa_ref[...].astype(F32)
    halo = gch_ref[...].astype(F32) * xah_ref[...].astype(F32)
    halo = jnp.where((i % tiles_per_seq) == 0, 0.0, halo)
    row = lax.broadcasted_iota(jnp.int32, u.shape, 0)
    h6 = halo[V7X_SUBLANES - 2:V7X_SUBLANES - 1]
    h7 = halo[V7X_SUBLANES - 1:V7X_SUBLANES]
    u1 = jnp.where(row >= 1, pltpu.roll(u, 1, axis=0), h7)
    u2 = jnp.where(row >= 2, pltpu.roll(u, 2, axis=0), jnp.where(row == 0, h6, h7))
    cw = cw_ref[...]
    conv = cw[0:1] * u2 + cw[1:2] * u1 + cw[2:3] * u
    gate = z_ref[...].astype(F32)
    half = CONV_WIDTH
    ya = (gb_ref[...].astype(F32) * conv * gate[:, :half]).astype(BF16)
    yb = (attn * gate[:, half:]).astype(BF16)
    acc = jnp.dot(ya, w_ref[0:half, :], preferred_element_type=F32)
    acc += jnp.dot(yb, w_ref[half:, :], preferred_element_type=F32)
    o_ref[...] = x_ref[...] + acc


def _outproj_even(proj, o_attn, conv_w, w_bf16, x2d, seq):
    t, d = x2d.shape
    tm = PROJ_TM
    w = CONV_WIDTH
    halo_blocks = tm // V7X_SUBLANES
    tiles_per_seq = seq // tm
    dil = o_attn.shape[1]

    def colblk(c, width=w):
        return pl.BlockSpec((tm, width), lambda i: (i, c))

    def halo(c):
        return pl.BlockSpec((V7X_SUBLANES, w), lambda i: (jnp.maximum(i * halo_blocks - 1, 0), c))

    return pl.pallas_call(
        functools.partial(_outproj_even_body, tiles_per_seq=tiles_per_seq),
        grid=(t // tm,),
        in_specs=[colblk(0), colblk(1), colblk(2), colblk(3, MIX_WIDTH),
                  pl.BlockSpec((None, dil, tm // dil, w), lambda i: (i // tiles_per_seq, 0, i % tiles_per_seq, 0)),
                  halo(1), halo(2),
                  pl.BlockSpec(conv_w.shape, lambda i: (0, 0)),
                  pl.BlockSpec(w_bf16.shape, lambda i: (0, 0)),
                  pl.BlockSpec((tm, d), lambda i: (i, 0))],
        out_specs=pl.BlockSpec((tm, d), lambda i: (i, 0)),
        out_shape=jax.ShapeDtypeStruct((t, d), F32),
        scratch_shapes=[pltpu.VMEM((N_SLABS, tm, V7X_LANES), F32)],
        compiler_params=_cparams(("parallel",)),
        name="conv_gate_outproj",
    )(proj, proj, proj, proj, o_attn, proj, proj, conv_w.astype(F32), w_bf16, x2d)


def _hgrn_body(q_ref, f_ref, i_ref, z_ref, og_ref, y_ref,
               st_ref, st0_ref, qrows_ref, frows_ref, krows_ref, *, rows):
    c = HGRN_CHUNK
    hd = HGRN_DIM
    heads = q_ref.shape[2] // hd
    chunks = rows // c
    t = pl.program_id(2)

    @pl.when(t == 0)
    def _():
        st_ref[...] = jnp.zeros_like(st_ref)

    st0_ref[...] = st_ref[...]
    ri = lax.broadcasted_iota(jnp.int32, (c, c), 0)
    ci = lax.broadcasted_iota(jnp.int32, (c, c), 1)
    causal = ri >= ci

    def lanes(h):
        return slice(h * hd, (h + 1) * hd)

    def finish(o, sl, h):
        ms = jnp.mean(o * o, axis=-1, keepdims=True)
        y = o * lax.rsqrt(ms + EPS) * og_ref[:, lanes(h)] * z_ref[0, sl, lanes(h)].astype(F32)
        y_ref[0, sl, lanes(h)] = y.astype(y_ref.dtype)

    units = [(h, n) for h in range(heads) for n in range(chunks)]
    worst = jnp.zeros((1, hd), F32)
    pre = {}
    tril = jnp.where(causal, 1.0, 0.0).astype(BF16)
    for h, n in units:
        sl = slice(n * c, (n + 1) * c)
        pre[h, n] = dict(gcum=jnp.dot(tril, f_ref[0, sl, lanes(h)], preferred_element_type=F32))
    for h, n in units:
        sl = slice(n * c, (n + 1) * c)
        d = pre[h, n]
        gcum = d["gcum"]
        q = q_ref[0, sl, lanes(h)].astype(F32)
        k = 1.0 - jnp.exp2(f_ref[0, sl, lanes(h)].astype(F32))
        rho = 0.5 * gcum[c - 1:c]
        worst = jnp.minimum(worst, jnp.where(rho == rho, rho, -jnp.inf))
        d.update(qt=(q * jnp.exp2(gcum - rho)).astype(BF16), kh=(k * jnp.exp2(rho - gcum)).astype(BF16),
                 e_rho=jnp.exp2(rho), v=i_ref[0, sl, lanes(h)])
    for u in units:
        pre[u]["dst"] = lax.dot_general(pre[u]["v"], pre[u]["kh"], (((0,), (0,)), ((), ())),
                                        preferred_element_type=F32) * pre[u]["e_rho"]
    for h in range(heads):
        st = st_ref[h]
        for n in range(chunks):
            d = pre[h, n]
            d["rhs"] = jnp.concatenate([d["kh"], (st * d["e_rho"]).astype(BF16)], axis=0)
            st = st * (d["e_rho"] * d["e_rho"]) + d["dst"]
        st_ref[h] = st
    for u in units:
        pre[u]["so"] = lax.dot_general(pre[u]["qt"], pre[u]["rhs"], (((1,), (1,)), ((), ())),
                                       preferred_element_type=F32)
    for u in units:
        scores = jnp.where(causal, pre[u]["so"][:, :c], 0.0).astype(BF16)
        pre[u]["o"] = pre[u]["so"][:, c:] + jnp.dot(scores, pre[u]["v"], preferred_element_type=F32)
    for h, n in units:
        finish(pre[h, n]["o"], slice(n * c, (n + 1) * c), h)

    @pl.when(jnp.min(worst) < -HGRN_MAX_HALF_DECAY)
    def _():
        lane = lax.broadcasted_iota(jnp.int32, (hd, c), 1)
        for h in range(heads):
            for n in range(chunks):
                sl = slice(n * c, (n + 1) * c)
                decay = jnp.exp2(f_ref[0, sl, lanes(h)].astype(F32))
                qrows_ref[...] = q_ref[0, sl, lanes(h)].astype(F32)
                frows_ref[...] = decay
                krows_ref[...] = 1.0 - decay
                vt = jnp.transpose(i_ref[0, sl, lanes(h)].astype(F32))

                def step(s, carry):
                    st, ot = carry
                    vcol = jnp.sum(jnp.where(lane == s, vt, 0.0), axis=-1, keepdims=True)
                    st = st * frows_ref[pl.ds(s, 1), :] + vcol * krows_ref[pl.ds(s, 1), :]
                    ocol = jnp.sum(st * qrows_ref[pl.ds(s, 1), :], axis=-1, keepdims=True)
                    ot = jnp.where(lane == s, ocol, ot)
                    return st, ot

                st, ot = lax.fori_loop(0, c, step, (st0_ref[h], jnp.zeros((hd, c), F32)))
                st0_ref[h] = st
                finish(jnp.transpose(ot), sl, h)
        st_ref[...] = st0_ref[...]


def _hgrn2(proj, o_gain, batch, seq):
    rows = HGRN_ROWS
    hd = HGRN_DIM
    heads = HGRN_HEADS_PER_STEP
    width = heads * hd
    projv = proj.reshape(batch, seq, IN_WIDTH)
    per_kind = HGRN_HEADS // heads

    def part(kind):
        return pl.BlockSpec((1, rows, width), lambda b, h, t: (b, t, kind * per_kind + h))

    def per_head():
        return pl.BlockSpec((1, width), lambda b, h, t: (0, h))

    y = pl.pallas_call(
        functools.partial(_hgrn_body, rows=rows),
        grid=(batch, per_kind, seq // rows),
        in_specs=[part(0), part(1), part(2), part(3), per_head()],
        out_specs=pl.BlockSpec((1, rows, width), lambda b, h, t: (b, t, h)),
        out_shape=jax.ShapeDtypeStruct((batch, seq, MIX_WIDTH), BF16),
        scratch_shapes=[pltpu.VMEM((heads, hd, hd), F32), pltpu.VMEM((heads, hd, hd), F32)]
        + [pltpu.VMEM((HGRN_CHUNK, hd), F32)] * 3,
        compiler_params=_cparams(("parallel", "parallel", "arbitrary")),
        name="hgrn2_recurrence",
    )(projv, projv, projv, projv, o_gain.reshape(1, -1).astype(F32))
    return y.reshape(batch * seq, MIX_WIDTH)


def _outproj_body(y_ref, w_ref, x_ref, o_ref):
    o_ref[...] = x_ref[...] + jnp.dot(y_ref[...], w_ref[...], preferred_element_type=F32)


def _outproj(y, w_bf16, x2d):
    t, d = x2d.shape
    tm = PROJ_TM
    return pl.pallas_call(
        _outproj_body,
        grid=(t // tm,),
        in_specs=[pl.BlockSpec((tm, y.shape[1]), lambda i: (i, 0)),
                  pl.BlockSpec(w_bf16.shape, lambda i: (0, 0)),
                  pl.BlockSpec((tm, d), lambda i: (i, 0))],
        out_specs=pl.BlockSpec((tm, d), lambda i: (i, 0)),
        out_shape=jax.ShapeDtypeStruct((t, d), F32),
        compiler_params=_cparams(("parallel",)),
        name="outproj_residual",
    )(y, w_bf16, x2d)


def kernel(x, ln_even, w_in_even, conv_w, q_gain, k_gain, w_out_even, rel_bias, ln_odd, w_in_odd,
           lower_bounds, o_gain, w_out_odd):
    batch, seq, d = x.shape
    assert d == D_MODEL and seq % (DILATED_PATTERNS[-1][1] * ATTN_BLOCK) == 0 and seq % PROJ_TM == 0
    depth = ln_even.shape[0] + ln_odd.shape[0]
    x2d = x.reshape(batch * seq, d).astype(F32)

    lbs = jnp.cumsum(jax.nn.softmax(lower_bounds.astype(F32), axis=0), axis=0)
    lbs = lbs - lbs[0:1]
    biases = [_band_bias(rel_bias, window, dilation) for window, dilation in DILATED_PATTERNS]
    ones = jnp.ones((1, IN_WIDTH), F32)
    q_tile, k_tile, _ = QKV_TILES

    for layer in range(depth):
        j = layer // 2
        if layer % 2 == 0:
            q_scale = jnp.tile(q_gain[j].astype(F32), ATTN_HEADS) * (ATTN_HEAD_DIM ** -0.5 * LOG2E)
            colscale = ones.at[0, q_tile * PROJ_TN:(q_tile + 1) * PROJ_TN].set(q_scale)
            colscale = colscale.at[0, k_tile * PROJ_TN:(k_tile + 1) * PROJ_TN].set(
                jnp.tile(k_gain[j].astype(F32), ATTN_HEADS))
            proj, qkv4, qkv16 = _norm_inproj(x2d, ln_even[j], w_in_even[j].astype(BF16), colscale, ones,
                                             EVEN_TILE_KINDS, False, batch, seq)
            prev = None
            for index, srcs in enumerate((proj, qkv4, qkv16)):
                prev = _attention_pattern(index, srcs, biases[index], prev, batch, seq)
            x2d = _outproj_even(proj, prev[0], conv_w[j], w_out_even[j].astype(BF16), x2d, seq)
        else:
            gate_cols = slice(ODD_TILE_KINDS.index("log2gate") * PROJ_TN,
                              (ODD_TILE_KINDS.index("log2gate") + ODD_TILE_KINDS.count("log2gate")) * PROJ_TN)
            gate_mid = ones.at[0, gate_cols].set(0.5 * (1.0 + lbs[j]))
            gate_half_range = ones.at[0, gate_cols].set(0.5 * (1.0 - lbs[j]))
            (proj,) = _norm_inproj(x2d, ln_odd[j], w_in_odd[j].astype(BF16), gate_mid, gate_half_range,
                                   ODD_TILE_KINDS, j == 0, batch, seq)
            y = _hgrn2(proj, o_gain[j], batch, seq)
            x2d = _outproj(y, w_out_odd[j].astype(BF16), x2d)
    return x2d.reshape(batch, seq, d).astype(x.dtype)
```

```python
import functools
import math

import jax
import jax.numpy as jnp
import numpy as np
from jax import lax
from jax.experimental import pallas as pl
from jax.experimental.pallas import tpu as pltpu

F32 = jnp.float32
BF16 = jnp.bfloat16

D_MODEL = 1024
MIX_WIDTH = 2 * D_MODEL
CONV_WIDTH = MIX_WIDTH // 2
ATTN_HEAD_DIM = 64
ATTN_HEADS = 16
ATTN_WIDTH = ATTN_HEADS * ATTN_HEAD_DIM
ATTN_BLOCK = 128
DILATED_PATTERNS = ((128, 1), (512, 4), (2048, 16))
DILATION_STEP = 4
REL_BUCKETS = 32
REL_MAX_DISTANCE = 2048
HGRN_HEADS = 16
HGRN_DIM = 128
IN_WIDTH = 8192
QKV_TILES = (3, 4, 5)
EPS = 1e-6
MASKED = -1e30
LOG2E = math.log2(math.e)

V7X_LANES = 128
V7X_SUBLANES = 8
V7X_VMEM_LIMIT_BYTES = 56 * 1024 * 1024

PROJ_TM = 512
INPROJ_TM_EVEN = 256
INPROJ_TM_ODD = 512
PROJ_TN = 1024
ATTN_ROWS = 512
ATTN_UNITS_PER_GROUP = 4
HGRN_ROWS = 512
HGRN_CHUNK = 128
HGRN_HEADS_PER_STEP = 4
HGRN_HEADS_PER_GROUP = 2
HGRN_MAX_HALF_DECAY = 100.0

N_SLABS = ATTN_WIDTH // V7X_LANES


def _cparams(semantics):
    return pltpu.CompilerParams(dimension_semantics=semantics, vmem_limit_bytes=V7X_VMEM_LIMIT_BYTES)


def _silu(t):
    half = 0.5 * t
    return half + half * jnp.tanh(half)


EVEN_TILE_KINDS = ("plain", "plain", "plain", "headnorm", "headnorm", "scatter", "silu", "silu")
ODD_TILE_KINDS = ("silu", "silu", "log2gate", "log2gate", "plain", "plain", "silu", "silu")


def _log2_forget_gate(f_pre, lb_terms):
    if lb_terms is None:
        log_f = jnp.minimum(f_pre, 0.0) - jnp.log(1.0 + jnp.exp(-jnp.abs(f_pre)))
    else:
        mid, half_range = lb_terms
        log_f = jnp.log(mid + half_range * jnp.tanh(0.5 * f_pre))
    return log_f * LOG2E


def _residue_major_permutation(rows, dilation):
    out = np.arange(rows)
    src = dilation * (out % (rows // dilation)) + out // (rows // dilation)
    return jnp.asarray(src[:, None] == np.arange(rows)[None, :], BF16)


def _inproj_body(x_ref, g_ref, w_ref, ca_ref, cb_ref, bd_ref, *rest, kinds, zero_lb):
    scatter = "scatter" in kinds
    if scatter:
        p4_ref, p16_ref, o_ref, o4_ref, o16_ref = rest
    else:
        (o_ref,) = rest
    x = x_ref[...]
    ms = jnp.mean(x * x, axis=-1, keepdims=True)
    hn = (x * lax.rsqrt(ms + EPS) * g_ref[...]).astype(BF16)
    tm = x.shape[0]
    tn = PROJ_TN

    def emit_residue_major(res, cols, part):
        res = res.astype(BF16)
        o_ref[:, cols] = res
        for perm_ref, ref in ((p4_ref, o4_ref), (p16_ref, o16_ref)):
            dil = ref.shape[0]
            moved = jnp.dot(perm_ref[...], res, preferred_element_type=F32).astype(ref.dtype)
            for r in range(dil):
                ref[r, :, part * tn:(part + 1) * tn] = moved[r * (tm // dil):(r + 1) * (tm // dil)]

    for jt, kind in enumerate(kinds):
        cols = slice(jt * tn, (jt + 1) * tn)
        acc = jnp.dot(hn, w_ref[:, cols], preferred_element_type=F32)
        if kind == "plain":
            o_ref[:, cols] = acc.astype(o_ref.dtype)
        elif kind == "silu":
            o_ref[:, cols] = _silu(acc).astype(o_ref.dtype)
        elif kind == "log2gate":
            lb_terms = None if zero_lb else (ca_ref[:, cols], cb_ref[:, cols])
            o_ref[:, cols] = _log2_forget_gate(acc, lb_terms).astype(o_ref.dtype)
        elif kind == "scatter":
            emit_residue_major(acc, cols, jt - QKV_TILES[0])
        else:
            width = bd_ref.shape[0]
            parts = []
            for s in range(tn // width):
                a = acc[:, s * width:(s + 1) * width]
                ms = jnp.dot((a * a).astype(BF16), bd_ref[...], preferred_element_type=F32)
                parts.append(a * lax.rsqrt(ms + EPS) * ca_ref[:, jt * tn + s * width:jt * tn + (s + 1) * width])
            emit_residue_major(jnp.concatenate(parts, axis=1), cols, jt - QKV_TILES[0])


def _norm_inproj(x2d, gain, w_bf16, col_a, col_b, kinds, zero_lb, batch, seq):
    t, d = x2d.shape
    n = w_bf16.shape[1]
    tm, tn = (INPROJ_TM_EVEN if "scatter" in kinds else INPROJ_TM_ODD), PROJ_TN
    assert len(kinds) == n // tn
    width = 2 * V7X_LANES
    head_of_lane = np.arange(width) // ATTN_HEAD_DIM
    bd = jnp.asarray((head_of_lane[:, None] == head_of_lane[None, :]) / ATTN_HEAD_DIM, BF16)
    tiles_per_seq = seq // tm
    out_specs = [pl.BlockSpec((tm, n), lambda i: (i, 0))]
    out_shape = [jax.ShapeDtypeStruct((t, n), BF16)]
    perms = []
    if "scatter" in kinds:
        qkv_width = len(QKV_TILES) * tn
        for _, dil in DILATED_PATTERNS[1:]:
            out_specs.append(pl.BlockSpec((None, dil, tm // dil, qkv_width),
                                          lambda i: (i // tiles_per_seq, 0, i % tiles_per_seq, 0)))
            out_shape.append(jax.ShapeDtypeStruct((batch, dil, seq // dil, qkv_width), BF16))
            perms.append(_residue_major_permutation(tm, dil))
    resident = dict(pipeline_mode=pl.Buffered(1))
    return pl.pallas_call(
        functools.partial(_inproj_body, kinds=kinds, zero_lb=zero_lb),
        grid=(t // tm,),
        in_specs=[
            pl.BlockSpec((tm, d), lambda i: (i, 0)),
            pl.BlockSpec((1, d), lambda i: (0, 0)),
            pl.BlockSpec((d, n), lambda i: (0, 0), **resident),
            pl.BlockSpec((1, n), lambda i: (0, 0)),
            pl.BlockSpec((1, n), lambda i: (0, 0)),
            pl.BlockSpec((width, width), lambda i: (0, 0)),
        ] + [pl.BlockSpec((tm, tm), lambda i: (0, 0))] * len(perms),
        out_specs=out_specs,
        out_shape=out_shape,
        compiler_params=_cparams(("parallel",)),
        name="norm_inproj_even" if "scatter" in kinds else "norm_inproj_odd",
    )(x2d, gain.reshape(1, d).astype(F32), w_bf16, col_a, col_b, bd, *perms)


def _attn_body(*refs, has_prev, scatter, rows):
    refs = list(refs)
    qkv_ref, pqkv_ref, bias_ref = refs[:3]
    del refs[:3]
    if has_prev:
        po_ref, pl_ref = refs[:2]
        del refs[:2]
    if scatter:
        o_ref, l_ref, so_ref, sl_ref = refs
    else:
        (o_ref,) = refs
    blk = ATTN_BLOCK
    width = ATTN_WIDTH
    t = pl.program_id(2)

    def window(part, qb, c0):
        cols = slice(part * width + c0, part * width + c0 + V7X_LANES)
        if qb == 0:
            return jnp.concatenate([pqkv_ref[:, cols], qkv_ref[0:blk, cols]], axis=0)
        return qkv_ref[(qb - 1) * blk:(qb + 1) * blk, cols]

    lane = lax.broadcasted_iota(jnp.int32, (blk, V7X_LANES), 1)
    low = lane < ATTN_HEAD_DIM
    low_bf = jnp.where(low, 1.0, 0.0).astype(BF16)
    high_bf = jnp.where(low, 0.0, 1.0).astype(BF16)
    low2_bf = jnp.concatenate([low_bf, low_bf], axis=0)
    high2_bf = jnp.concatenate([high_bf, high_bf], axis=0)
    n_pairs = ATTN_HEADS // 2
    step = DILATION_STEP
    sub = blk // step

    def locate(idx):
        qb = idx // n_pairs
        hp = idx % n_pairs
        return qb, hp, qb * blk, hp * V7X_LANES

    def logits(idx):
        qb, hp, r0, c0 = locate(idx)
        q2 = qkv_ref[r0:r0 + blk, c0:c0 + V7X_LANES]
        kw = window(1, qb, c0)
        qs = jnp.concatenate([q2 * low_bf, q2 * high_bf], axis=0)
        return lax.dot_general(qs, kw, (((1,), (1,)), ((), ())), preferred_element_type=F32)

    def softmax(idx, s):
        qb, hp, _, _ = locate(idx)
        first = (t == 0).astype(jnp.int32) if qb == 0 else 0
        s = s + jnp.concatenate([bias_ref[first, 2 * hp], bias_ref[first, 2 * hp + 1]], axis=0)
        m = jnp.max(s, axis=-1, keepdims=True)
        p = jnp.exp2(s - m).astype(BF16)
        return p, jnp.where(low, m[0:blk], m[blk:2 * blk])

    def weighted(idx, p):
        qb, _, _, c0 = locate(idx)
        vw = window(2, qb, c0)
        rhs = jnp.concatenate([jnp.concatenate([vw * low2_bf, low2_bf], axis=1),
                               jnp.concatenate([vw * high2_bf, high2_bf], axis=1)], axis=0)
        lhs = jnp.concatenate([p[0:blk], p[blk:2 * blk]], axis=1)
        return jnp.dot(lhs, rhs, preferred_element_type=F32)

    def finish(idx, slot, pv, m2):
        qb, _, r0, c0 = locate(idx)
        den = pv[:, V7X_LANES:]
        o2 = pv[:, 0:V7X_LANES] / den
        if has_prev or scatter:
            lse2 = m2 + jnp.log(den) * LOG2E
        if has_prev:
            po = po_ref[pl.ds(r0, blk), pl.ds(c0, V7X_LANES)].astype(F32)
            plse = pl_ref[pl.ds(r0, blk), pl.ds(c0, V7X_LANES)]
            gap = plse - lse2
            e = jnp.exp2(-jnp.abs(gap))
            prev_larger = gap >= 0.0
            tot = 1.0 + e
            o2 = (jnp.where(prev_larger, po, o2) + e * jnp.where(prev_larger, o2, po)) / tot
            lse2 = jnp.maximum(plse, lse2) + jnp.log(tot) * LOG2E
        if not scatter:
            o_ref[pl.ds(r0, blk), pl.ds(c0, V7X_LANES)] = o2.astype(o_ref.dtype)
            return
        so_ref[slot] = o2
        sl_ref[slot] = lse2
        d0 = qb * sub
        for jm in range(step):
            o_ref[jm, pl.ds(d0, sub), pl.ds(c0, V7X_LANES)] = so_ref[
                slot, pl.ds(jm, sub, stride=step), :].astype(o_ref.dtype)
            l_ref[jm, pl.ds(d0, sub), pl.ds(c0, V7X_LANES)] = sl_ref[slot, pl.ds(jm, sub, stride=step), :]

    per_group = ATTN_UNITS_PER_GROUP
    n_units = (rows // blk) * n_pairs
    groups = [list(range(g, g + per_group)) for g in range(0, n_units, per_group)]
    pending = None
    for units in groups + [None]:
        pvs = None
        if pending is not None:
            prev_units, probs = pending
            pvs = [weighted(u, p) for u, (p, _) in zip(prev_units, probs)]
        scores = [logits(u) for u in units] if units is not None else None
        if pvs is not None:
            for slot, (u, pv, (_, m2)) in enumerate(zip(prev_units, pvs, probs)):
                finish(u, slot, pv, m2)
        pending = (units, [softmax(u, s) for u, s in zip(units, scores)]) if units is not None else None


def _attention_pattern(index, srcs, bias, prev, batch, seq):
    dilation = DILATED_PATTERNS[index][1]
    last = index == len(DILATED_PATTERNS) - 1
    length = seq // dilation
    rows = min(ATTN_ROWS, length)
    blk = ATTN_BLOCK
    width = ATTN_WIDTH
    per_step = rows // blk

    qkv_width = len(QKV_TILES) * width
    if index == 0:
        src = srcs.reshape(batch, seq, IN_WIDTH)
        assert QKV_TILES[0] * width == qkv_width
        cur = pl.BlockSpec((None, rows, qkv_width), lambda b, r, t: (b, t, 1))
        prv = pl.BlockSpec((None, blk, qkv_width), lambda b, r, t: (b, jnp.maximum(t * per_step - 1, 0), 1))
    else:
        src = srcs
        cur = pl.BlockSpec((None, None, rows, qkv_width), lambda b, r, t: (b, r, t, 0))
        prv = pl.BlockSpec((None, None, blk, qkv_width),
                           lambda b, r, t: (b, r, jnp.maximum(t * per_step - 1, 0), 0))

    res_blk = pl.BlockSpec((None, None, rows, width), lambda b, r, t: (b, r, t, 0))
    in_specs = [cur, prv, pl.BlockSpec(bias.shape, lambda b, r, t: (0, 0, 0, 0))]
    args = [src, src, bias]
    if prev is not None:
        in_specs += [res_blk, res_blk]
        args += [prev[0].reshape(batch, dilation, length, width), prev[1].reshape(batch, dilation, length, width)]
    scratch = []
    if last:
        out_specs = [res_blk]
        out_shape = [jax.ShapeDtypeStruct((batch, dilation, length, width), BF16)]
    else:
        step = DILATION_STEP
        shape = (batch, step, dilation, length // step, width)
        out_blk = pl.BlockSpec((None, step, None, rows // step, width), lambda b, r, t: (b, 0, r, t, 0))
        out_specs = [out_blk, out_blk]
        out_shape = [jax.ShapeDtypeStruct(shape, BF16), jax.ShapeDtypeStruct(shape, F32)]
        scratch += [pltpu.VMEM((ATTN_UNITS_PER_GROUP, blk, V7X_LANES), F32)] * 2
    return pl.pallas_call(
        functools.partial(_attn_body, has_prev=prev is not None, scatter=not last, rows=rows),
        grid=(batch, dilation, length // rows),
        in_specs=in_specs,
        out_specs=out_specs,
        out_shape=out_shape,
        scratch_shapes=scratch,
        compiler_params=_cparams(("parallel", "parallel", "arbitrary")),
        name=f"dilated_attention_d{dilation}",
    )(*args)


def _t5_bucket(distance):
    max_exact = REL_BUCKETS // 2
    scaled = jnp.log(jnp.maximum(distance, max_exact).astype(F32) / max_exact) / math.log(
        REL_MAX_DISTANCE / max_exact)
    large = jnp.minimum(max_exact + (scaled * (REL_BUCKETS - max_exact)).astype(jnp.int32), REL_BUCKETS - 1)
    return jnp.where(distance < max_exact, distance, large)


def _band_bias(rel_bias, window, dilation):
    blk = ATTN_BLOCK
    n_back = window // dilation
    qi = jnp.arange(blk)[:, None]
    kj = jnp.arange(2 * blk)[None, :]
    delta = blk + qi - kj
    band = (delta >= 0) & (delta <= n_back)
    onehot = jax.nn.one_hot(_t5_bucket(jnp.maximum(delta, 0) * dilation), REL_BUCKETS, dtype=F32)
    bias = jnp.einsum("qkb,bh->hqk", onehot, rel_bias.astype(F32) * LOG2E, precision=lax.Precision.HIGHEST)
    bias = jnp.where(band[None], bias, MASKED)
    return jnp.stack([bias, jnp.where(kj[None] >= blk, bias, MASKED)])


def _outproj_even_body(gb_ref, gc_ref, xa_ref, z_ref, oa_ref, gch_ref, xah_ref, cw_ref, w_ref, x_ref,
                       o_ref, slab_ref, *, tiles_per_seq):
    i = pl.program_id(0)
    tm = gc_ref.shape[0]
    sub = V7X_SUBLANES
    half = CONV_WIDTH
    dil = oa_ref.shape[0]
    kc = 2 * V7X_LANES
    acc = x_ref[...]
    for c0 in range(0, half, kc):
        for s in range(c0 // V7X_LANES, (c0 + kc) // V7X_LANES):
            for r in range(dil):
                slab_ref[s, pl.ds(r, tm // dil, stride=dil), :] = oa_ref[
                    r, :, s * V7X_LANES:(s + 1) * V7X_LANES].astype(F32)
        attn = jnp.concatenate([slab_ref[s] for s in range(c0 // V7X_LANES, (c0 + kc) // V7X_LANES)], axis=1)
        yb = attn.astype(BF16) * z_ref[:, half + c0:half + c0 + kc]
        acc += jnp.dot(yb, w_ref[half + c0:half + c0 + kc, :], preferred_element_type=F32)

    first = (i % tiles_per_seq) == 0
    row = lax.broadcasted_iota(jnp.int32, (sub, kc), 0)
    for c0 in range(0, half, kc):
        cols = slice(c0, c0 + kc)
        u = gc_ref[:, cols].astype(F32) * xa_ref[:, cols].astype(F32)
        halo = jnp.where(first, 0.0, gch_ref[:, cols].astype(F32) * xah_ref[:, cols].astype(F32))
        r1 = pltpu.roll(u, 1, axis=0)
        r2 = pltpu.roll(u, 2, axis=0)
        u1 = jnp.concatenate([jnp.where(row >= 1, r1[0:sub], pltpu.roll(halo, 1, axis=0)), r1[sub:]], axis=0)
        u2 = jnp.concatenate([jnp.where(row >= 2, r2[0:sub], pltpu.roll(halo, 2, axis=0)), r2[sub:]], axis=0)
        conv = cw_ref[0:1, cols] * u2 + cw_ref[1:2, cols] * u1 + cw_ref[2:3, cols] * u
        ya = (gb_ref[:, cols].astype(F32) * conv * z_ref[:, cols].astype(F32)).astype(BF16)
        acc += jnp.dot(ya, w_ref[cols, :], preferred_element_type=F32)
    o_ref[...] = acc


def _outproj_even(proj, o_attn, conv_w, w_bf16, x2d, seq):
    t, d = x2d.shape
    tm = PROJ_TM
    w = CONV_WIDTH
    halo_blocks = tm // V7X_SUBLANES
    tiles_per_seq = seq // tm
    dil = o_attn.shape[1]

    def colblk(c, width=w):
        return pl.BlockSpec((tm, width), lambda i: (i, c))

    def halo(c):
        return pl.BlockSpec((V7X_SUBLANES, w), lambda i: (jnp.maximum(i * halo_blocks - 1, 0), c))

    return pl.pallas_call(
        functools.partial(_outproj_even_body, tiles_per_seq=tiles_per_seq),
        grid=(t // tm,),
        in_specs=[colblk(0), colblk(1), colblk(2), colblk(3, MIX_WIDTH),
                  pl.BlockSpec((None, dil, tm // dil, w), lambda i: (i // tiles_per_seq, 0, i % tiles_per_seq, 0)),
                  halo(1), halo(2),
                  pl.BlockSpec(conv_w.shape, lambda i: (0, 0)),
                  pl.BlockSpec(w_bf16.shape, lambda i: (0, 0)),
                  pl.BlockSpec((tm, d), lambda i: (i, 0))],
        out_specs=pl.BlockSpec((tm, d), lambda i: (i, 0)),
        out_shape=jax.ShapeDtypeStruct((t, d), F32),
        scratch_shapes=[pltpu.VMEM((N_SLABS, tm, V7X_LANES), F32)],
        compiler_params=_cparams(("parallel",)),
        name="conv_gate_outproj",
    )(proj, proj, proj, proj, o_attn, proj, proj, conv_w.astype(F32), w_bf16, x2d)


def _hgrn_body(q_ref, f_ref, i_ref, z_ref, og_ref, y_ref,
               st_ref, st0_ref, qrows_ref, frows_ref, krows_ref, *, rows):
    c = HGRN_CHUNK
    hd = HGRN_DIM
    heads = q_ref.shape[2] // hd
    chunks = rows // c
    t = pl.program_id(2)

    @pl.when(t == 0)
    def _():
        st_ref[...] = jnp.zeros_like(st_ref)

    st0_ref[...] = st_ref[...]
    ri = lax.broadcasted_iota(jnp.int32, (c, c), 0)
    ci = lax.broadcasted_iota(jnp.int32, (c, c), 1)
    causal = ri >= ci

    def lanes(h):
        return slice(h * hd, (h + 1) * hd)

    def finish(o, sl, h):
        ms = jnp.mean(o * o, axis=-1, keepdims=True)
        y = o * lax.rsqrt(ms + EPS) * og_ref[:, lanes(h)] * z_ref[0, sl, lanes(h)].astype(F32)
        y_ref[0, sl, lanes(h)] = y.astype(y_ref.dtype)

    worst = [jnp.zeros((1, hd), F32)]
    pre = {}
    tril = jnp.where(causal, 1.0, 0.0).astype(BF16)

    def stage_prefix(units):
        for h, n in units:
            sl = slice(n * c, (n + 1) * c)
            pre[h, n] = dict(gcum=jnp.dot(tril, f_ref[0, sl, lanes(h)], preferred_element_type=F32))

    def stage_scale(units):
        for h, n in units:
            sl = slice(n * c, (n + 1) * c)
            d = pre[h, n]
            gcum = d["gcum"]
            q = q_ref[0, sl, lanes(h)].astype(F32)
            k = 1.0 - jnp.exp2(f_ref[0, sl, lanes(h)].astype(F32))
            rho = 0.5 * gcum[c - 1:c]
            worst[0] = jnp.minimum(worst[0], jnp.where(rho == rho, rho, -jnp.inf))
            d.update(qt=(q * jnp.exp2(gcum - rho)).astype(BF16), kh=(k * jnp.exp2(rho - gcum)).astype(BF16),
                     e_rho=jnp.exp2(rho), v=i_ref[0, sl, lanes(h)])

    def stage_state(units):
        for u in units:
            pre[u]["dst"] = lax.dot_general(pre[u]["v"], pre[u]["kh"], (((0,), (0,)), ((), ())),
                                            preferred_element_type=F32) * pre[u]["e_rho"]
        for h in sorted({h for h, _ in units}):
            st = st_ref[h]
            for n in range(chunks):
                d = pre[h, n]
                d["rhs"] = jnp.concatenate([d["kh"], (st * d["e_rho"]).astype(BF16)], axis=0)
                st = st * (d["e_rho"] * d["e_rho"]) + d["dst"]
            st_ref[h] = st

    def stage_scores(units):
        for u in units:
            pre[u]["so"] = lax.dot_general(pre[u]["qt"], pre[u]["rhs"], (((1,), (1,)), ((), ())),
                                           preferred_element_type=F32)

    def stage_output(units):
        for u in units:
            scores = jnp.where(causal, pre[u]["so"][:, :c], 0.0).astype(BF16)
            pre[u]["o"] = pre[u]["so"][:, c:] + jnp.dot(scores, pre[u]["v"], preferred_element_type=F32)
        for h, n in units:
            finish(pre[h, n]["o"], slice(n * c, (n + 1) * c), h)

    gsize = HGRN_HEADS_PER_GROUP
    groups = [[(h, n) for h in range(g, g + gsize) for n in range(chunks)] for g in range(0, heads, gsize)]
    stages = (stage_prefix, stage_scale, stage_state, stage_scores, stage_output)
    for tick in range(len(stages) + len(groups) - 1):
        for gi, units in enumerate(groups):
            if 0 <= tick - gi < len(stages):
                stages[tick - gi](units)
    worst = worst[0]

    @pl.when(jnp.min(worst) < -HGRN_MAX_HALF_DECAY)
    def _():
        lane = lax.broadcasted_iota(jnp.int32, (hd, c), 1)
        for h in range(heads):
            for n in range(chunks):
                sl = slice(n * c, (n + 1) * c)
                decay = jnp.exp2(f_ref[0, sl, lanes(h)].astype(F32))
                qrows_ref[...] = q_ref[0, sl, lanes(h)].astype(F32)
                frows_ref[...] = decay
                krows_ref[...] = 1.0 - decay
                vt = jnp.transpose(i_ref[0, sl, lanes(h)].astype(F32))

                def step(s, carry):
                    st, ot = carry
                    vcol = jnp.sum(jnp.where(lane == s, vt, 0.0), axis=-1, keepdims=True)
                    st = st * frows_ref[pl.ds(s, 1), :] + vcol * krows_ref[pl.ds(s, 1), :]
                    ocol = jnp.sum(st * qrows_ref[pl.ds(s, 1), :], axis=-1, keepdims=True)
                    ot = jnp.where(lane == s, ocol, ot)
                    return st, ot

                st, ot = lax.fori_loop(0, c, step, (st0_ref[h], jnp.zeros((hd, c), F32)))
                st0_ref[h] = st
                finish(jnp.transpose(ot), sl, h)
        st_ref[...] = st0_ref[...]


def _hgrn2(proj, o_gain, batch, seq):
    rows = HGRN_ROWS
    hd = HGRN_DIM
    heads = HGRN_HEADS_PER_STEP
    width = heads * hd
    projv = proj.reshape(batch, seq, IN_WIDTH)
    per_kind = HGRN_HEADS // heads

    def part(kind):
        return pl.BlockSpec((1, rows, width), lambda b, h, t: (b, t, kind * per_kind + h))

    def per_head():
        return pl.BlockSpec((1, width), lambda b, h, t: (0, h))

    y = pl.pallas_call(
        functools.partial(_hgrn_body, rows=rows),
        grid=(batch, per_kind, seq // rows),
        in_specs=[part(0), part(1), part(2), part(3), per_head()],
        out_specs=pl.BlockSpec((1, rows, width), lambda b, h, t: (b, t, h)),
        out_shape=jax.ShapeDtypeStruct((batch, seq, MIX_WIDTH), BF16),
        scratch_shapes=[pltpu.VMEM((heads, hd, hd), F32), pltpu.VMEM((heads, hd, hd), F32)]
        + [pltpu.VMEM((HGRN_CHUNK, hd), F32)] * 3,
        compiler_params=_cparams(("parallel", "parallel", "arbitrary")),
        name="hgrn2_recurrence",
    )(projv, projv, projv, projv, o_gain.reshape(1, -1).astype(F32))
    return y.reshape(batch * seq, MIX_WIDTH)


def _outproj_body(y_ref, w_ref, x_ref, o_ref):
    o_ref[...] = x_ref[...] + jnp.dot(y_ref[...], w_ref[...], preferred_element_type=F32)


def _outproj(y, w_bf16, x2d):
    t, d = x2d.shape
    tm = PROJ_TM
    return pl.pallas_call(
        _outproj_body,
        grid=(t // tm,),
        in_specs=[pl.BlockSpec((tm, y.shape[1]), lambda i: (i, 0)),
                  pl.BlockSpec(w_bf16.shape, lambda i: (0, 0)),
                  pl.BlockSpec((tm, d), lambda i: (i, 0))],
        out_specs=pl.BlockSpec((tm, d), lambda i: (i, 0)),
        out_shape=jax.ShapeDtypeStruct((t, d), F32),
        compiler_params=_cparams(("parallel",)),
        name="outproj_residual",
    )(y, w_bf16, x2d)


def kernel(x, ln_even, w_in_even, conv_w, q_gain, k_gain, w_out_even, rel_bias, ln_odd, w_in_odd,
           lower_bounds, o_gain, w_out_odd):
    batch, seq, d = x.shape
    assert d == D_MODEL and seq % (DILATED_PATTERNS[-1][1] * ATTN_BLOCK) == 0 and seq % PROJ_TM == 0
    depth = ln_even.shape[0] + ln_odd.shape[0]
    x2d = x.reshape(batch * seq, d).astype(F32)

    lbs = jnp.cumsum(jax.nn.softmax(lower_bounds.astype(F32), axis=0), axis=0)
    lbs = lbs - lbs[0:1]
    biases = [_band_bias(rel_bias, window, dilation) for window, dilation in DILATED_PATTERNS]
    ones = jnp.ones((1, IN_WIDTH), F32)
    q_tile, k_tile, _ = QKV_TILES

    for layer in range(depth):
        j = layer // 2
        if layer % 2 == 0:
            q_scale = jnp.tile(q_gain[j].astype(F32), ATTN_HEADS) * (ATTN_HEAD_DIM ** -0.5 * LOG2E)
            colscale = ones.at[0, q_tile * PROJ_TN:(q_tile + 1) * PROJ_TN].set(q_scale)
            colscale = colscale.at[0, k_tile * PROJ_TN:(k_tile + 1) * PROJ_TN].set(
                jnp.tile(k_gain[j].astype(F32), ATTN_HEADS))
            proj, qkv4, qkv16 = _norm_inproj(x2d, ln_even[j], w_in_even[j].astype(BF16), colscale, ones,
                                             EVEN_TILE_KINDS, False, batch, seq)
            prev = None
            for index, srcs in enumerate((proj, qkv4, qkv16)):
                prev = _attention_pattern(index, srcs, biases[index], prev, batch, seq)
            x2d = _outproj_even(proj, prev[0], conv_w[j], w_out_even[j].astype(BF16), x2d, seq)
        else:
            gate_cols = slice(ODD_TILE_KINDS.index("log2gate") * PROJ_TN,
                              (ODD_TILE_KINDS.index("log2gate") + ODD_TILE_KINDS.count("log2gate")) * PROJ_TN)
            gate_mid = ones.at[0, gate_cols].set(0.5 * (1.0 + lbs[j]))
            gate_half_range = ones.at[0, gate_cols].set(0.5 * (1.0 - lbs[j]))
            (proj,) = _norm_inproj(x2d, ln_odd[j], w_in_odd[j].astype(BF16), gate_mid, gate_half_range,
                                   ODD_TILE_KINDS, j == 0, batch, seq)
            y = _hgrn2(proj, o_gain[j], batch, seq)
            x2d = _outproj(y, w_out_odd[j].astype(BF16), x2d)
    return x2d.reshape(batch, seq, d).astype(x.dtype)
```

```python
import functools
import math

import jax
import jax.numpy as jnp
import numpy as np
from jax import lax
from jax.experimental import pallas as pl
from jax.experimental.pallas import tpu as pltpu

F32 = jnp.float32
BF16 = jnp.bfloat16

D_MODEL = 1024
MIX_WIDTH = 2 * D_MODEL
CONV_WIDTH = MIX_WIDTH // 2
ATTN_HEAD_DIM = 64
ATTN_HEADS = 16
ATTN_WIDTH = ATTN_HEADS * ATTN_HEAD_DIM
ATTN_BLOCK = 128
DILATED_PATTERNS = ((128, 1), (512, 4), (2048, 16))
DILATION_STEP = 4
REL_BUCKETS = 32
REL_MAX_DISTANCE = 2048
HGRN_HEADS = 16
HGRN_DIM = 128
IN_WIDTH = 8192
EPS = 1e-6
MASKED = -1e30
LOG2E = math.log2(math.e)

V7X_LANES = 128
V7X_SUBLANES = 8
V7X_VMEM_LIMIT_BYTES = 56 * 1024 * 1024

PROJ_TM = 512
INPROJ_TM_EVEN = 256
INPROJ_TM_ODD = 512
PROJ_TN = 1024
ATTN_ROWS = 512
ATTN_UNITS_PER_GROUP = 4
HGRN_ROWS = 512
HGRN_CHUNK = 128
HGRN_HEADS_PER_STEP = 4
HGRN_HEADS_PER_GROUP = 2
HGRN_MAX_HALF_DECAY = 100.0

N_SLABS = ATTN_WIDTH // V7X_LANES


def _cparams(semantics):
    return pltpu.CompilerParams(dimension_semantics=semantics, vmem_limit_bytes=V7X_VMEM_LIMIT_BYTES)


def _silu(t):
    half = 0.5 * t
    return half + half * jnp.tanh(half)


EVEN_PLAN = (("headnorm", 3, 0), ("headnorm", 4, 1), ("scatter", 5, 2),
             ("gate_product", (0, 6), 3),
             ("product", (1, 2), 4),
             ("silu", 7, 5))
ODD_PLAN = (("silu", 0, 0), ("silu", 1, 1), ("log2gate", 2, 2), ("log2gate", 3, 3),
            ("plain", 4, 4), ("plain", 5, 5), ("silu", 6, 6), ("silu", 7, 7))
QKV_OUT_TILES = 3
EVEN_OUT_WIDTH = len(EVEN_PLAN) * PROJ_TN
G1_TILE, U_TILE, Z2_TILE = 3, 4, 5


def _log2_forget_gate(f_pre, lb_terms):
    if lb_terms is None:
        log_f = jnp.minimum(f_pre, 0.0) - jnp.log(1.0 + jnp.exp(-jnp.abs(f_pre)))
    else:
        mid, half_range = lb_terms
        log_f = jnp.log(mid + half_range * jnp.tanh(0.5 * f_pre))
    return log_f * LOG2E


def _residue_major_permutation(rows, dilation):
    out = np.arange(rows)
    src = dilation * (out % (rows // dilation)) + out // (rows // dilation)
    return jnp.asarray(src[:, None] == np.arange(rows)[None, :], BF16)


def _inproj_body(x_ref, g_ref, w_ref, ca_ref, cb_ref, bd_ref, *rest, plan, zero_lb):
    scatter = any(kind == "scatter" for kind, _, _ in plan)
    if scatter:
        p4_ref, p16_ref, o_ref, o4_ref, o16_ref = rest
    else:
        (o_ref,) = rest
    x = x_ref[...]
    ms = jnp.mean(x * x, axis=-1, keepdims=True)
    hn = (x * lax.rsqrt(ms + EPS) * g_ref[...]).astype(BF16)
    tm = x.shape[0]
    tn = PROJ_TN

    def project(jt):
        return jnp.dot(hn, w_ref[:, jt * tn:(jt + 1) * tn], preferred_element_type=F32)

    def emit_residue_major(res, cols, part):
        res = res.astype(BF16)
        o_ref[:, cols] = res
        for perm_ref, ref in ((p4_ref, o4_ref), (p16_ref, o16_ref)):
            dil = ref.shape[0]
            moved = jnp.dot(perm_ref[...], res, preferred_element_type=F32).astype(ref.dtype)
            for r in range(dil):
                ref[r, :, part * tn:(part + 1) * tn] = moved[r * (tm // dil):(r + 1) * (tm // dil)]

    for kind, jt, out_tile in plan:
        cols = slice(out_tile * tn, (out_tile + 1) * tn)
        if kind == "product":
            o_ref[:, cols] = (project(jt[0]) * project(jt[1])).astype(o_ref.dtype)
        elif kind == "gate_product":
            o_ref[:, cols] = (project(jt[0]) * _silu(project(jt[1]))).astype(o_ref.dtype)
        elif kind == "plain":
            o_ref[:, cols] = project(jt).astype(o_ref.dtype)
        elif kind == "silu":
            o_ref[:, cols] = _silu(project(jt)).astype(o_ref.dtype)
        elif kind == "log2gate":
            wcols = slice(jt * tn, (jt + 1) * tn)
            lb_terms = None if zero_lb else (ca_ref[:, wcols], cb_ref[:, wcols])
            o_ref[:, cols] = _log2_forget_gate(project(jt), lb_terms).astype(o_ref.dtype)
        elif kind == "scatter":
            emit_residue_major(project(jt), cols, out_tile)
        else:
            acc = project(jt)
            width = bd_ref.shape[0]
            parts = []
            for s in range(tn // width):
                a = acc[:, s * width:(s + 1) * width]
                ms = jnp.dot((a * a).astype(BF16), bd_ref[...], preferred_element_type=F32)
                parts.append(a * lax.rsqrt(ms + EPS) * ca_ref[:, jt * tn + s * width:jt * tn + (s + 1) * width])
            emit_residue_major(jnp.concatenate(parts, axis=1), cols, out_tile)


def _norm_inproj(x2d, gain, w_bf16, col_a, col_b, plan, zero_lb, batch, seq):
    t, d = x2d.shape
    n = w_bf16.shape[1]
    scatter = any(kind == "scatter" for kind, _, _ in plan)
    tm, tn = (INPROJ_TM_EVEN if scatter else INPROJ_TM_ODD), PROJ_TN
    n_out = len(plan) * tn
    width = 2 * V7X_LANES
    head_of_lane = np.arange(width) // ATTN_HEAD_DIM
    bd = jnp.asarray((head_of_lane[:, None] == head_of_lane[None, :]) / ATTN_HEAD_DIM, BF16)
    tiles_per_seq = seq // tm
    out_specs = [pl.BlockSpec((tm, n_out), lambda i: (i, 0))]
    out_shape = [jax.ShapeDtypeStruct((t, n_out), BF16)]
    perms = []
    if scatter:
        qkv_width = QKV_OUT_TILES * tn
        for _, dil in DILATED_PATTERNS[1:]:
            out_specs.append(pl.BlockSpec((None, dil, tm // dil, qkv_width),
                                          lambda i: (i // tiles_per_seq, 0, i % tiles_per_seq, 0)))
            out_shape.append(jax.ShapeDtypeStruct((batch, dil, seq // dil, qkv_width), BF16))
            perms.append(_residue_major_permutation(tm, dil))
    resident = dict(pipeline_mode=pl.Buffered(1))
    return pl.pallas_call(
        functools.partial(_inproj_body, plan=plan, zero_lb=zero_lb),
        grid=(t // tm,),
        in_specs=[
            pl.BlockSpec((tm, d), lambda i: (i, 0)),
            pl.BlockSpec((1, d), lambda i: (0, 0)),
            pl.BlockSpec((d, n), lambda i: (0, 0), **resident),
            pl.BlockSpec((1, n), lambda i: (0, 0)),
            pl.BlockSpec((1, n), lambda i: (0, 0)),
            pl.BlockSpec((width, width), lambda i: (0, 0)),
        ] + [pl.BlockSpec((tm, tm), lambda i: (0, 0))] * len(perms),
        out_specs=out_specs,
        out_shape=out_shape,
        compiler_params=_cparams(("parallel",)),
        name="norm_inproj_even" if scatter else "norm_inproj_odd",
    )(x2d, gain.reshape(1, d).astype(F32), w_bf16, col_a, col_b, bd, *perms)


def _attn_body(*refs, has_prev, scatter, rows):
    refs = list(refs)
    qkv_ref, pqkv_ref, bias_ref = refs[:3]
    del refs[:3]
    if has_prev:
        po_ref, pl_ref = refs[:2]
        del refs[:2]
    if scatter:
        o_ref, l_ref, so_ref, sl_ref = refs
    else:
        (o_ref,) = refs
    blk = ATTN_BLOCK
    width = ATTN_WIDTH
    t = pl.program_id(2)

    def window(part, qb, c0):
        cols = slice(part * width + c0, part * width + c0 + V7X_LANES)
        if qb == 0:
            return jnp.concatenate([pqkv_ref[:, cols], qkv_ref[0:blk, cols]], axis=0)
        return qkv_ref[(qb - 1) * blk:(qb + 1) * blk, cols]

    lane = lax.broadcasted_iota(jnp.int32, (blk, V7X_LANES), 1)
    low = lane < ATTN_HEAD_DIM
    low_bf = jnp.where(low, 1.0, 0.0).astype(BF16)
    high_bf = jnp.where(low, 0.0, 1.0).astype(BF16)
    low2_bf = jnp.concatenate([low_bf, low_bf], axis=0)
    high2_bf = jnp.concatenate([high_bf, high_bf], axis=0)
    n_pairs = ATTN_HEADS // 2
    step = DILATION_STEP
    sub = blk // step

    def locate(idx):
        qb = idx // n_pairs
        hp = idx % n_pairs
        return qb, hp, qb * blk, hp * V7X_LANES

    def logits(idx):
        qb, hp, r0, c0 = locate(idx)
        q2 = qkv_ref[r0:r0 + blk, c0:c0 + V7X_LANES]
        kw = window(1, qb, c0)
        qs = jnp.concatenate([q2 * low_bf, q2 * high_bf], axis=0)
        return lax.dot_general(qs, kw, (((1,), (1,)), ((), ())), preferred_element_type=F32)

    def softmax(idx, s):
        qb, hp, _, _ = locate(idx)
        first = (t == 0).astype(jnp.int32) if qb == 0 else 0
        s = s + jnp.concatenate([bias_ref[first, 2 * hp], bias_ref[first, 2 * hp + 1]], axis=0)
        m = jnp.max(s, axis=-1, keepdims=True)
        p = jnp.exp2(s - m).astype(BF16)
        return p, jnp.where(low, m[0:blk], m[blk:2 * blk])

    def weighted(idx, p):
        qb, _, _, c0 = locate(idx)
        vw = window(2, qb, c0)
        rhs = jnp.concatenate([jnp.concatenate([vw * low2_bf, low2_bf], axis=1),
                               jnp.concatenate([vw * high2_bf, high2_bf], axis=1)], axis=0)
        lhs = jnp.concatenate([p[0:blk], p[blk:2 * blk]], axis=1)
        return jnp.dot(lhs, rhs, preferred_element_type=F32)

    def finish(idx, slot, pv, m2):
        qb, _, r0, c0 = locate(idx)
        den = pv[:, V7X_LANES:]
        o2 = pv[:, 0:V7X_LANES] / den
        if has_prev or scatter:
            lse2 = m2 + jnp.log(den) * LOG2E
        if has_prev:
            po = po_ref[pl.ds(r0, blk), pl.ds(c0, V7X_LANES)].astype(F32)
            plse = pl_ref[pl.ds(r0, blk), pl.ds(c0, V7X_LANES)]
            gap = plse - lse2
            e = jnp.exp2(-jnp.abs(gap))
            prev_larger = gap >= 0.0
            tot = 1.0 + e
            o2 = (jnp.where(prev_larger, po, o2) + e * jnp.where(prev_larger, o2, po)) / tot
            lse2 = jnp.maximum(plse, lse2) + jnp.log(tot) * LOG2E
        if not scatter:
            o_ref[pl.ds(r0, blk), pl.ds(c0, V7X_LANES)] = o2.astype(o_ref.dtype)
            return
        so_ref[slot] = o2
        sl_ref[slot] = lse2
        d0 = qb * sub
        for jm in range(step):
            o_ref[jm, pl.ds(d0, sub), pl.ds(c0, V7X_LANES)] = so_ref[
                slot, pl.ds(jm, sub, stride=step), :].astype(o_ref.dtype)
            l_ref[jm, pl.ds(d0, sub), pl.ds(c0, V7X_LANES)] = sl_ref[slot, pl.ds(jm, sub, stride=step), :]

    per_group = ATTN_UNITS_PER_GROUP
    n_units = (rows // blk) * n_pairs
    groups = [list(range(g, g + per_group)) for g in range(0, n_units, per_group)]
    pending = None
    for units in groups + [None]:
        pvs = None
        if pending is not None:
            prev_units, probs = pending
            pvs = [weighted(u, p) for u, (p, _) in zip(prev_units, probs)]
        scores = [logits(u) for u in units] if units is not None else None
        if pvs is not None:
            for slot, (u, pv, (_, m2)) in enumerate(zip(prev_units, pvs, probs)):
                finish(u, slot, pv, m2)
        pending = (units, [softmax(u, s) for u, s in zip(units, scores)]) if units is not None else None


def _attention_pattern(index, srcs, bias, prev, batch, seq):
    dilation = DILATED_PATTERNS[index][1]
    last = index == len(DILATED_PATTERNS) - 1
    length = seq // dilation
    rows = min(ATTN_ROWS, length)
    blk = ATTN_BLOCK
    width = ATTN_WIDTH
    per_step = rows // blk

    qkv_width = QKV_OUT_TILES * width
    if index == 0:
        src = srcs.reshape(batch, seq, EVEN_OUT_WIDTH)
        cur = pl.BlockSpec((None, rows, qkv_width), lambda b, r, t: (b, t, 0))
        prv = pl.BlockSpec((None, blk, qkv_width), lambda b, r, t: (b, jnp.maximum(t * per_step - 1, 0), 0))
    else:
        src = srcs
        cur = pl.BlockSpec((None, None, rows, qkv_width), lambda b, r, t: (b, r, t, 0))
        prv = pl.BlockSpec((None, None, blk, qkv_width),
                           lambda b, r, t: (b, r, jnp.maximum(t * per_step - 1, 0), 0))

    res_blk = pl.BlockSpec((None, None, rows, width), lambda b, r, t: (b, r, t, 0))
    in_specs = [cur, prv, pl.BlockSpec(bias.shape, lambda b, r, t: (0, 0, 0, 0))]
    args = [src, src, bias]
    if prev is not None:
        in_specs += [res_blk, res_blk]
        args += [prev[0].reshape(batch, dilation, length, width), prev[1].reshape(batch, dilation, length, width)]
    scratch = []
    if last:
        out_specs = [res_blk]
        out_shape = [jax.ShapeDtypeStruct((batch, dilation, length, width), BF16)]
    else:
        step = DILATION_STEP
        shape = (batch, step, dilation, length // step, width)
        out_blk = pl.BlockSpec((None, step, None, rows // step, width), lambda b, r, t: (b, 0, r, t, 0))
        out_specs = [out_blk, out_blk]
        out_shape = [jax.ShapeDtypeStruct(shape, BF16), jax.ShapeDtypeStruct(shape, F32)]
        scratch += [pltpu.VMEM((ATTN_UNITS_PER_GROUP, blk, V7X_LANES), F32)] * 2
    return pl.pallas_call(
        functools.partial(_attn_body, has_prev=prev is not None, scatter=not last, rows=rows),
        grid=(batch, dilation, length // rows),
        in_specs=in_specs,
        out_specs=out_specs,
        out_shape=out_shape,
        scratch_shapes=scratch,
        compiler_params=_cparams(("parallel", "parallel", "arbitrary")),
        name=f"dilated_attention_d{dilation}",
    )(*args)


def _t5_bucket(distance):
    max_exact = REL_BUCKETS // 2
    scaled = jnp.log(jnp.maximum(distance, max_exact).astype(F32) / max_exact) / math.log(
        REL_MAX_DISTANCE / max_exact)
    large = jnp.minimum(max_exact + (scaled * (REL_BUCKETS - max_exact)).astype(jnp.int32), REL_BUCKETS - 1)
    return jnp.where(distance < max_exact, distance, large)


def _band_bias(rel_bias, window, dilation):
    blk = ATTN_BLOCK
    n_back = window // dilation
    qi = jnp.arange(blk)[:, None]
    kj = jnp.arange(2 * blk)[None, :]
    delta = blk + qi - kj
    band = (delta >= 0) & (delta <= n_back)
    onehot = jax.nn.one_hot(_t5_bucket(jnp.maximum(delta, 0) * dilation), REL_BUCKETS, dtype=F32)
    bias = jnp.einsum("qkb,bh->hqk", onehot, rel_bias.astype(F32) * LOG2E, precision=lax.Precision.HIGHEST)
    bias = jnp.where(band[None], bias, MASKED)
    return jnp.stack([bias, jnp.where(kj[None] >= blk, bias, MASKED)])


def _outproj_even_body(g1_ref, u_ref, z2_ref, oa_ref, uh_ref, cw_ref, w_ref, x_ref, o_ref, slab_ref, *,
                       tiles_per_seq):
    i = pl.program_id(0)
    tm = u_ref.shape[0]
    sub = V7X_SUBLANES
    half = CONV_WIDTH
    dil = oa_ref.shape[0]
    kc = 2 * V7X_LANES
    acc = x_ref[...]
    for c0 in range(0, half, kc):
        for s in range(c0 // V7X_LANES, (c0 + kc) // V7X_LANES):
            for r in range(dil):
                slab_ref[s, pl.ds(r, tm // dil, stride=dil), :] = oa_ref[
                    r, :, s * V7X_LANES:(s + 1) * V7X_LANES].astype(F32)
        attn = jnp.concatenate([slab_ref[s] for s in range(c0 // V7X_LANES, (c0 + kc) // V7X_LANES)], axis=1)
        yb = attn.astype(BF16) * z2_ref[:, c0:c0 + kc]
        acc += jnp.dot(yb, w_ref[half + c0:half + c0 + kc, :], preferred_element_type=F32)

    first = (i % tiles_per_seq) == 0
    row = lax.broadcasted_iota(jnp.int32, (sub, kc), 0)
    for c0 in range(0, half, kc):
        cols = slice(c0, c0 + kc)
        u = u_ref[:, cols].astype(F32)
        halo = jnp.where(first, 0.0, uh_ref[:, cols].astype(F32))
        r1 = pltpu.roll(u, 1, axis=0)
        r2 = pltpu.roll(u, 2, axis=0)
        u1 = jnp.concatenate([jnp.where(row >= 1, r1[0:sub], pltpu.roll(halo, 1, axis=0)), r1[sub:]], axis=0)
        u2 = jnp.concatenate([jnp.where(row >= 2, r2[0:sub], pltpu.roll(halo, 2, axis=0)), r2[sub:]], axis=0)
        conv = cw_ref[0:1, cols] * u2 + cw_ref[1:2, cols] * u1 + cw_ref[2:3, cols] * u
        ya = (g1_ref[:, cols].astype(F32) * conv).astype(BF16)
        acc += jnp.dot(ya, w_ref[cols, :], preferred_element_type=F32)
    o_ref[...] = acc


def _outproj_even(proj, o_attn, conv_w, w_bf16, x2d, seq):
    t, d = x2d.shape
    tm = PROJ_TM
    w = CONV_WIDTH
    halo_blocks = tm // V7X_SUBLANES
    tiles_per_seq = seq // tm
    dil = o_attn.shape[1]

    def colblk(c):
        return pl.BlockSpec((tm, w), lambda i: (i, c))

    return pl.pallas_call(
        functools.partial(_outproj_even_body, tiles_per_seq=tiles_per_seq),
        grid=(t // tm,),
        in_specs=[colblk(G1_TILE), colblk(U_TILE), colblk(Z2_TILE),
                  pl.BlockSpec((None, dil, tm // dil, w), lambda i: (i // tiles_per_seq, 0, i % tiles_per_seq, 0)),
                  pl.BlockSpec((V7X_SUBLANES, w), lambda i: (jnp.maximum(i * halo_blocks - 1, 0), U_TILE)),
                  pl.BlockSpec(conv_w.shape, lambda i: (0, 0)),
                  pl.BlockSpec(w_bf16.shape, lambda i: (0, 0)),
                  pl.BlockSpec((tm, d), lambda i: (i, 0))],
        out_specs=pl.BlockSpec((tm, d), lambda i: (i, 0)),
        out_shape=jax.ShapeDtypeStruct((t, d), F32),
        scratch_shapes=[pltpu.VMEM((N_SLABS, tm, V7X_LANES), F32)],
        compiler_params=_cparams(("parallel",)),
        name="conv_gate_outproj",
    )(proj, proj, proj, o_attn, proj, conv_w.astype(F32), w_bf16, x2d)


def _hgrn_body(q_ref, f_ref, i_ref, z_ref, og_ref, y_ref,
               st_ref, st0_ref, qrows_ref, frows_ref, krows_ref, *, rows):
    c = HGRN_CHUNK
    hd = HGRN_DIM
    heads = q_ref.shape[2] // hd
    chunks = rows // c
    t = pl.program_id(2)

    @pl.when(t == 0)
    def _():
        st_ref[...] = jnp.zeros_like(st_ref)

    st0_ref[...] = st_ref[...]
    ri = lax.broadcasted_iota(jnp.int32, (c, c), 0)
    ci = lax.broadcasted_iota(jnp.int32, (c, c), 1)
    causal = ri >= ci

    def lanes(h):
        return slice(h * hd, (h + 1) * hd)

    def finish(o, sl, h):
        ms = jnp.mean(o * o, axis=-1, keepdims=True)
        y = o * lax.rsqrt(ms + EPS) * og_ref[:, lanes(h)] * z_ref[0, sl, lanes(h)].astype(F32)
        y_ref[0, sl, lanes(h)] = y.astype(y_ref.dtype)

    worst = [jnp.zeros((1, hd), F32)]
    pre = {}
    tril = jnp.where(causal, 1.0, 0.0).astype(BF16)

    def stage_prefix(units):
        for h, n in units:
            sl = slice(n * c, (n + 1) * c)
            pre[h, n] = dict(gcum=jnp.dot(tril, f_ref[0, sl, lanes(h)], preferred_element_type=F32))

    def stage_scale(units):
        for h, n in units:
            sl = slice(n * c, (n + 1) * c)
            d = pre[h, n]
            gcum = d["gcum"]
            q = q_ref[0, sl, lanes(h)].astype(F32)
            k = 1.0 - jnp.exp2(f_ref[0, sl, lanes(h)].astype(F32))
            rho = 0.5 * gcum[c - 1:c]
            worst[0] = jnp.minimum(worst[0], jnp.where(rho == rho, rho, -jnp.inf))
            d.update(qt=(q * jnp.exp2(gcum - rho)).astype(BF16), kh=(k * jnp.exp2(rho - gcum)).astype(BF16),
                     e_rho=jnp.exp2(rho), v=i_ref[0, sl, lanes(h)])

    def stage_state(units):
        for u in units:
            pre[u]["dst"] = lax.dot_general(pre[u]["v"], pre[u]["kh"], (((0,), (0,)), ((), ())),
                                            preferred_element_type=F32) * pre[u]["e_rho"]
        for h in sorted({h for h, _ in units}):
            st = st_ref[h]
            for n in range(chunks):
                d = pre[h, n]
                d["rhs"] = jnp.concatenate([d["kh"], (st * d["e_rho"]).astype(BF16)], axis=0)
                st = st * (d["e_rho"] * d["e_rho"]) + d["dst"]
            st_ref[h] = st

    def stage_scores(units):
        for u in units:
            pre[u]["so"] = lax.dot_general(pre[u]["qt"], pre[u]["rhs"], (((1,), (1,)), ((), ())),
                                           preferred_element_type=F32)

    def stage_output(units):
        for u in units:
            scores = jnp.where(causal, pre[u]["so"][:, :c], 0.0).astype(BF16)
            pre[u]["o"] = pre[u]["so"][:, c:] + jnp.dot(scores, pre[u]["v"], preferred_element_type=F32)
        for h, n in units:
            finish(pre[h, n]["o"], slice(n * c, (n + 1) * c), h)

    gsize = HGRN_HEADS_PER_GROUP
    groups = [[(h, n) for h in range(g, g + gsize) for n in range(chunks)] for g in range(0, heads, gsize)]
    stages = (stage_prefix, stage_scale, stage_state, stage_scores, stage_output)
    for tick in range(len(stages) + len(groups) - 1):
        for gi, units in enumerate(groups):
            if 0 <= tick - gi < len(stages):
                stages[tick - gi](units)
    worst = worst[0]

    @pl.when(jnp.min(worst) < -HGRN_MAX_HALF_DECAY)
    def _():
        lane = lax.broadcasted_iota(jnp.int32, (hd, c), 1)
        for h in range(heads):
            for n in range(chunks):
                sl = slice(n * c, (n + 1) * c)
                decay = jnp.exp2(f_ref[0, sl, lanes(h)].astype(F32))
                qrows_ref[...] = q_ref[0, sl, lanes(h)].astype(F32)
                frows_ref[...] = decay
                krows_ref[...] = 1.0 - decay
                vt = jnp.transpose(i_ref[0, sl, lanes(h)].astype(F32))

                def step(s, carry):
                    st, ot = carry
                    vcol = jnp.sum(jnp.where(lane == s, vt, 0.0), axis=-1, keepdims=True)
                    st = st * frows_ref[pl.ds(s, 1), :] + vcol * krows_ref[pl.ds(s, 1), :]
                    ocol = jnp.sum(st * qrows_ref[pl.ds(s, 1), :], axis=-1, keepdims=True)
                    ot = jnp.where(lane == s, ocol, ot)
                    return st, ot

                st, ot = lax.fori_loop(0, c, step, (st0_ref[h], jnp.zeros((hd, c), F32)))
                st0_ref[h] = st
                finish(jnp.transpose(ot), sl, h)
        st_ref[...] = st0_ref[...]


def _hgrn2(proj, o_gain, batch, seq):
    rows = HGRN_ROWS
    hd = HGRN_DIM
    heads = HGRN_HEADS_PER_STEP
    width = heads * hd
    projv = proj.reshape(batch, seq, IN_WIDTH)
    per_kind = HGRN_HEADS // heads

    def part(kind):
        return pl.BlockSpec((1, rows, width), lambda b, h, t: (b, t, kind * per_kind + h))

    def per_head():
        return pl.BlockSpec((1, width), lambda b, h, t: (0, h))

    y = pl.pallas_call(
        functools.partial(_hgrn_body, rows=rows),
        grid=(batch, per_kind, seq // rows),
        in_specs=[part(0), part(1), part(2), part(3), per_head()],
        out_specs=pl.BlockSpec((1, rows, width), lambda b, h, t: (b, t, h)),
        out_shape=jax.ShapeDtypeStruct((batch, seq, MIX_WIDTH), BF16),
        scratch_shapes=[pltpu.VMEM((heads, hd, hd), F32), pltpu.VMEM((heads, hd, hd), F32)]
        + [pltpu.VMEM((HGRN_CHUNK, hd), F32)] * 3,
        compiler_params=_cparams(("parallel", "parallel", "arbitrary")),
        name="hgrn2_recurrence",
    )(projv, projv, projv, projv, o_gain.reshape(1, -1).astype(F32))
    return y.reshape(batch * seq, MIX_WIDTH)


def _outproj_body(y_ref, w_ref, x_ref, o_ref):
    o_ref[...] = x_ref[...] + jnp.dot(y_ref[...], w_ref[...], preferred_element_type=F32)


def _outproj(y, w_bf16, x2d):
    t, d = x2d.shape
    tm = PROJ_TM
    return pl.pallas_call(
        _outproj_body,
        grid=(t // tm,),
        in_specs=[pl.BlockSpec((tm, y.shape[1]), lambda i: (i, 0)),
                  pl.BlockSpec(w_bf16.shape, lambda i: (0, 0)),
                  pl.BlockSpec((tm, d), lambda i: (i, 0))],
        out_specs=pl.BlockSpec((tm, d), lambda i: (i, 0)),
        out_shape=jax.ShapeDtypeStruct((t, d), F32),
        compiler_params=_cparams(("parallel",)),
        name="outproj_residual",
    )(y, w_bf16, x2d)


def kernel(x, ln_even, w_in_even, conv_w, q_gain, k_gain, w_out_even, rel_bias, ln_odd, w_in_odd,
           lower_bounds, o_gain, w_out_odd):
    batch, seq, d = x.shape
    assert d == D_MODEL and seq % (DILATED_PATTERNS[-1][1] * ATTN_BLOCK) == 0 and seq % PROJ_TM == 0
    depth = ln_even.shape[0] + ln_odd.shape[0]
    x2d = x.reshape(batch * seq, d).astype(F32)

    lbs = jnp.cumsum(jax.nn.softmax(lower_bounds.astype(F32), axis=0), axis=0)
    lbs = lbs - lbs[0:1]
    biases = [_band_bias(rel_bias, window, dilation) for window, dilation in DILATED_PATTERNS]
    ones = jnp.ones((1, IN_WIDTH), F32)
    q_tile, k_tile = EVEN_PLAN[0][1], EVEN_PLAN[1][1]

    for layer in range(depth):
        j = layer // 2
        if layer % 2 == 0:
            q_scale = jnp.tile(q_gain[j].astype(F32), ATTN_HEADS) * (ATTN_HEAD_DIM ** -0.5 * LOG2E)
            colscale = ones.at[0, q_tile * PROJ_TN:(q_tile + 1) * PROJ_TN].set(q_scale)
            colscale = colscale.at[0, k_tile * PROJ_TN:(k_tile + 1) * PROJ_TN].set(
                jnp.tile(k_gain[j].astype(F32), ATTN_HEADS))
            proj, qkv4, qkv16 = _norm_inproj(x2d, ln_even[j], w_in_even[j].astype(BF16), colscale, ones,
                                             EVEN_PLAN, False, batch, seq)
            prev = None
            for index, srcs in enumerate((proj, qkv4, qkv16)):
                prev = _attention_pattern(index, srcs, biases[index], prev, batch, seq)
            x2d = _outproj_even(proj, prev[0], conv_w[j], w_out_even[j].astype(BF16), x2d, seq)
        else:
            gate_tiles = [jt for kind, jt, _ in ODD_PLAN if kind == "log2gate"]
            gate_cols = slice(min(gate_tiles) * PROJ_TN, (max(gate_tiles) + 1) * PROJ_TN)
            gate_mid = ones.at[0, gate_cols].set(0.5 * (1.0 + lbs[j]))
            gate_half_range = ones.at[0, gate_cols].set(0.5 * (1.0 - lbs[j]))
            (proj,) = _norm_inproj(x2d, ln_odd[j], w_in_odd[j].astype(BF16), gate_mid, gate_half_range,
                                   ODD_PLAN, j == 0, batch, seq)
            y = _hgrn2(proj, o_gain[j], batch, seq)
            x2d = _outproj(y, w_out_odd[j].astype(BF16), x2d)
    return x2d.reshape(batch, seq, d).astype(x.dtype)
```

```python
import functools
import math

import jax
import jax.numpy as jnp
import numpy as np
from jax import lax
from jax.experimental import pallas as pl
from jax.experimental.pallas import tpu as pltpu

F32 = jnp.float32
BF16 = jnp.bfloat16

D_MODEL = 1024
MIX_WIDTH = 2 * D_MODEL
CONV_WIDTH = MIX_WIDTH // 2
ATTN_HEAD_DIM = 64
ATTN_HEADS = 16
ATTN_WIDTH = ATTN_HEADS * ATTN_HEAD_DIM
ATTN_BLOCK = 128
DILATED_PATTERNS = ((128, 1), (512, 4), (2048, 16))
DILATION_STEP = 4
REL_BUCKETS = 32
REL_MAX_DISTANCE = 2048
HGRN_HEADS = 16
HGRN_DIM = 128
IN_WIDTH = 8192
EPS = 1e-6
MASKED = -1e30
LOG2E = math.log2(math.e)

V7X_LANES = 128
V7X_SUBLANES = 8
V7X_VMEM_LIMIT_BYTES = 56 * 1024 * 1024

PROJ_TM = 512
INPROJ_TM_EVEN = 256
INPROJ_TM_ODD = 512
PROJ_TN = 1024
CAST_BLOCK_ELEMS = 2 * 1024 * 1024
ATTN_ROWS = 512
ATTN_UNITS_PER_GROUP = 4
HGRN_ROWS = 512
HGRN_CHUNK = 128
HGRN_HEADS_PER_STEP = 4
HGRN_HEADS_PER_GROUP = 2
HGRN_MAX_HALF_DECAY = 100.0

N_SLABS = ATTN_WIDTH // V7X_LANES


def _cparams(semantics):
    return pltpu.CompilerParams(dimension_semantics=semantics, vmem_limit_bytes=V7X_VMEM_LIMIT_BYTES)


def _cast_body(w_ref, o_ref):
    o_ref[...] = w_ref[...].astype(o_ref.dtype)


def _to_bf16(w):
    layers, k, n = w.shape
    tn = min(n, CAST_BLOCK_ELEMS // k)
    return pl.pallas_call(
        _cast_body,
        grid=(layers, n // tn),
        in_specs=[pl.BlockSpec((None, k, tn), lambda l, j: (l, 0, j))],
        out_specs=pl.BlockSpec((None, k, tn), lambda l, j: (l, 0, j)),
        out_shape=jax.ShapeDtypeStruct(w.shape, BF16),
        compiler_params=_cparams(("parallel", "parallel")),
        name="weights_to_bf16",
    )(w.astype(F32))


def _silu(t):
    half = 0.5 * t
    return half + half * jnp.tanh(half)


EVEN_PLAN = (("headnorm", 3, 0), ("headnorm", 4, 1), ("scatter", 5, 2),
             ("gate_product", (0, 6), 3),
             ("product", (1, 2), 4),
             ("silu", 7, 5))
ODD_PLAN = (("silu", 0, 0), ("silu", 1, 1), ("log2gate", 2, 2), ("log2gate", 3, 3),
            ("plain", 4, 4), ("plain", 5, 5), ("silu", 6, 6), ("silu", 7, 7))
QKV_OUT_TILES = 3
EVEN_OUT_WIDTH = len(EVEN_PLAN) * PROJ_TN
G1_TILE, U_TILE, Z2_TILE = 3, 4, 5


def _log2_forget_gate(f_pre, lb_terms):
    if lb_terms is None:
        log_f = jnp.minimum(f_pre, 0.0) - jnp.log(1.0 + jnp.exp(-jnp.abs(f_pre)))
    else:
        mid, half_range = lb_terms
        log_f = jnp.log(mid + half_range * jnp.tanh(0.5 * f_pre))
    return log_f * LOG2E


def _residue_major_permutation(rows, dilation):
    out = np.arange(rows)
    src = dilation * (out % (rows // dilation)) + out // (rows // dilation)
    return jnp.asarray(src[:, None] == np.arange(rows)[None, :], BF16)


def _inproj_body(x_ref, g_ref, w_ref, ca_ref, cb_ref, bd_ref, *rest, plan, zero_lb):
    scatter = any(kind == "scatter" for kind, _, _ in plan)
    if scatter:
        p4_ref, p16_ref, o_ref, o4_ref, o16_ref = rest
    else:
        (o_ref,) = rest
    x = x_ref[...]
    ms = jnp.mean(x * x, axis=-1, keepdims=True)
    hn = (x * lax.rsqrt(ms + EPS) * g_ref[...]).astype(BF16)
    tm = x.shape[0]
    tn = PROJ_TN

    def project(jt):
        return jnp.dot(hn, w_ref[:, jt * tn:(jt + 1) * tn], preferred_element_type=F32)

    def emit_residue_major(res, cols, part):
        res = res.astype(BF16)
        o_ref[:, cols] = res
        for perm_ref, ref in ((p4_ref, o4_ref), (p16_ref, o16_ref)):
            dil = ref.shape[0]
            moved = jnp.dot(perm_ref[...], res, preferred_element_type=F32).astype(ref.dtype)
            for r in range(dil):
                ref[r, :, part * tn:(part + 1) * tn] = moved[r * (tm // dil):(r + 1) * (tm // dil)]

    for kind, jt, out_tile in plan:
        cols = slice(out_tile * tn, (out_tile + 1) * tn)
        if kind == "product":
            o_ref[:, cols] = (project(jt[0]) * project(jt[1])).astype(o_ref.dtype)
        elif kind == "gate_product":
            o_ref[:, cols] = (project(jt[0]) * _silu(project(jt[1]))).astype(o_ref.dtype)
        elif kind == "plain":
            o_ref[:, cols] = project(jt).astype(o_ref.dtype)
        elif kind == "silu":
            o_ref[:, cols] = _silu(project(jt)).astype(o_ref.dtype)
        elif kind == "log2gate":
            wcols = slice(jt * tn, (jt + 1) * tn)
            lb_terms = None if zero_lb else (ca_ref[:, wcols], cb_ref[:, wcols])
            o_ref[:, cols] = _log2_forget_gate(project(jt), lb_terms).astype(o_ref.dtype)
        elif kind == "scatter":
            emit_residue_major(project(jt), cols, out_tile)
        else:
            acc = project(jt)
            width = bd_ref.shape[0]
            parts = []
            for s in range(tn // width):
                a = acc[:, s * width:(s + 1) * width]
                ms = jnp.dot((a * a).astype(BF16), bd_ref[...], preferred_element_type=F32)
                parts.append(a * lax.rsqrt(ms + EPS) * ca_ref[:, jt * tn + s * width:jt * tn + (s + 1) * width])
            emit_residue_major(jnp.concatenate(parts, axis=1), cols, out_tile)


def _norm_inproj(x2d, gain, w_bf16, col_a, col_b, plan, zero_lb, batch, seq):
    t, d = x2d.shape
    n = w_bf16.shape[1]
    scatter = any(kind == "scatter" for kind, _, _ in plan)
    tm, tn = (INPROJ_TM_EVEN if scatter else INPROJ_TM_ODD), PROJ_TN
    n_out = len(plan) * tn
    width = 2 * V7X_LANES
    head_of_lane = np.arange(width) // ATTN_HEAD_DIM
    bd = jnp.asarray((head_of_lane[:, None] == head_of_lane[None, :]) / ATTN_HEAD_DIM, BF16)
    tiles_per_seq = seq // tm
    out_specs = [pl.BlockSpec((tm, n_out), lambda i: (i, 0))]
    out_shape = [jax.ShapeDtypeStruct((t, n_out), BF16)]
    perms = []
    if scatter:
        qkv_width = QKV_OUT_TILES * tn
        for _, dil in DILATED_PATTERNS[1:]:
            out_specs.append(pl.BlockSpec((None, dil, tm // dil, qkv_width),
                                          lambda i: (i // tiles_per_seq, 0, i % tiles_per_seq, 0)))
            out_shape.append(jax.ShapeDtypeStruct((batch, dil, seq // dil, qkv_width), BF16))
            perms.append(_residue_major_permutation(tm, dil))
    resident = dict(pipeline_mode=pl.Buffered(1))
    return pl.pallas_call(
        functools.partial(_inproj_body, plan=plan, zero_lb=zero_lb),
        grid=(t // tm,),
        in_specs=[
            pl.BlockSpec((tm, d), lambda i: (i, 0)),
            pl.BlockSpec((1, d), lambda i: (0, 0)),
            pl.BlockSpec((d, n), lambda i: (0, 0), **resident),
            pl.BlockSpec((1, n), lambda i: (0, 0)),
            pl.BlockSpec((1, n), lambda i: (0, 0)),
            pl.BlockSpec((width, width), lambda i: (0, 0)),
        ] + [pl.BlockSpec((tm, tm), lambda i: (0, 0))] * len(perms),
        out_specs=out_specs,
        out_shape=out_shape,
        compiler_params=_cparams(("parallel",)),
        name="norm_inproj_even" if scatter else "norm_inproj_odd",
    )(x2d, gain.reshape(1, d).astype(F32), w_bf16, col_a, col_b, bd, *perms)


def _attn_body(*refs, has_prev, scatter, rows):
    refs = list(refs)
    qkv_ref, pqkv_ref, bias_ref = refs[:3]
    del refs[:3]
    if has_prev:
        po_ref, pl_ref = refs[:2]
        del refs[:2]
    if scatter:
        o_ref, l_ref, so_ref, sl_ref = refs
    else:
        (o_ref,) = refs
    blk = ATTN_BLOCK
    width = ATTN_WIDTH
    t = pl.program_id(2)

    def window(part, qb, c0):
        cols = slice(part * width + c0, part * width + c0 + V7X_LANES)
        if qb == 0:
            return jnp.concatenate([pqkv_ref[:, cols], qkv_ref[0:blk, cols]], axis=0)
        return qkv_ref[(qb - 1) * blk:(qb + 1) * blk, cols]

    lane = lax.broadcasted_iota(jnp.int32, (blk, V7X_LANES), 1)
    low = lane < ATTN_HEAD_DIM
    low_bf = jnp.where(low, 1.0, 0.0).astype(BF16)
    high_bf = jnp.where(low, 0.0, 1.0).astype(BF16)
    low2_bf = jnp.concatenate([low_bf, low_bf], axis=0)
    high2_bf = jnp.concatenate([high_bf, high_bf], axis=0)
    n_pairs = ATTN_HEADS // 2
    step = DILATION_STEP
    sub = blk // step

    def locate(idx):
        qb = idx // n_pairs
        hp = idx % n_pairs
        return qb, hp, qb * blk, hp * V7X_LANES

    def logits(idx):
        qb, hp, r0, c0 = locate(idx)
        q2 = qkv_ref[r0:r0 + blk, c0:c0 + V7X_LANES]
        kw = window(1, qb, c0)
        qs = jnp.concatenate([q2 * low_bf, q2 * high_bf], axis=0)
        return lax.dot_general(qs, kw, (((1,), (1,)), ((), ())), preferred_element_type=F32)

    def softmax(idx, s):
        qb, hp, _, _ = locate(idx)
        first = (t == 0).astype(jnp.int32) if qb == 0 else 0
        s = s + jnp.concatenate([bias_ref[first, 2 * hp], bias_ref[first, 2 * hp + 1]], axis=0)
        m = jnp.max(s, axis=-1, keepdims=True)
        p = jnp.exp2(s - m).astype(BF16)
        return p, jnp.where(low, m[0:blk], m[blk:2 * blk])

    def weighted(idx, p):
        qb, _, _, c0 = locate(idx)
        vw = window(2, qb, c0)
        rhs = jnp.concatenate([jnp.concatenate([vw * low2_bf, low2_bf], axis=1),
                               jnp.concatenate([vw * high2_bf, high2_bf], axis=1)], axis=0)
        lhs = jnp.concatenate([p[0:blk], p[blk:2 * blk]], axis=1)
        return jnp.dot(lhs, rhs, preferred_element_type=F32)

    def finish(idx, slot, pv, m2):
        qb, _, r0, c0 = locate(idx)
        den = pv[:, V7X_LANES:]
        o2 = pv[:, 0:V7X_LANES] / den
        if has_prev or scatter:
            lse2 = m2 + jnp.log(den) * LOG2E
        if has_prev:
            po = po_ref[pl.ds(r0, blk), pl.ds(c0, V7X_LANES)].astype(F32)
            plse = pl_ref[pl.ds(r0, blk), pl.ds(c0, V7X_LANES)]
            gap = plse - lse2
            e = jnp.exp2(-jnp.abs(gap))
            prev_larger = gap >= 0.0
            tot = 1.0 + e
            o2 = (jnp.where(prev_larger, po, o2) + e * jnp.where(prev_larger, o2, po)) / tot
            lse2 = jnp.maximum(plse, lse2) + jnp.log(tot) * LOG2E
        if not scatter:
            o_ref[pl.ds(r0, blk), pl.ds(c0, V7X_LANES)] = o2.astype(o_ref.dtype)
            return
        so_ref[slot] = o2
        sl_ref[slot] = lse2
        d0 = qb * sub
        for jm in range(step):
            o_ref[jm, pl.ds(d0, sub), pl.ds(c0, V7X_LANES)] = so_ref[
                slot, pl.ds(jm, sub, stride=step), :].astype(o_ref.dtype)
            l_ref[jm, pl.ds(d0, sub), pl.ds(c0, V7X_LANES)] = sl_ref[slot, pl.ds(jm, sub, stride=step), :]

    per_group = ATTN_UNITS_PER_GROUP
    n_units = (rows // blk) * n_pairs
    groups = [list(range(g, g + per_group)) for g in range(0, n_units, per_group)]
    pending = None
    for units in groups + [None]:
        pvs = None
        if pending is not None:
            prev_units, probs = pending
            pvs = [weighted(u, p) for u, (p, _) in zip(prev_units, probs)]
        scores = [logits(u) for u in units] if units is not None else None
        if pvs is not None:
            for slot, (u, pv, (_, m2)) in enumerate(zip(prev_units, pvs, probs)):
                finish(u, slot, pv, m2)
        pending = (units, [softmax(u, s) for u, s in zip(units, scores)]) if units is not None else None


def _attention_pattern(index, srcs, bias, prev, batch, seq):
    dilation = DILATED_PATTERNS[index][1]
    last = index == len(DILATED_PATTERNS) - 1
    length = seq // dilation
    rows = min(ATTN_ROWS, length)
    blk = ATTN_BLOCK
    width = ATTN_WIDTH
    per_step = rows // blk

    qkv_width = QKV_OUT_TILES * width
    if index == 0:
        src = srcs.reshape(batch, seq, EVEN_OUT_WIDTH)
        cur = pl.BlockSpec((None, rows, qkv_width), lambda b, r, t: (b, t, 0))
        prv = pl.BlockSpec((None, blk, qkv_width), lambda b, r, t: (b, jnp.maximum(t * per_step - 1, 0), 0))
    else:
        src = srcs
        cur = pl.BlockSpec((None, None, rows, qkv_width), lambda b, r, t: (b, r, t, 0))
        prv = pl.BlockSpec((None, None, blk, qkv_width),
                           lambda b, r, t: (b, r, jnp.maximum(t * per_step - 1, 0), 0))

    res_blk = pl.BlockSpec((None, None, rows, width), lambda b, r, t: (b, r, t, 0))
    in_specs = [cur, prv, pl.BlockSpec(bias.shape, lambda b, r, t: (0, 0, 0, 0))]
    args = [src, src, bias]
    if prev is not None:
        in_specs += [res_blk, res_blk]
        args += [prev[0].reshape(batch, dilation, length, width), prev[1].reshape(batch, dilation, length, width)]
    scratch = []
    if last:
        out_specs = [res_blk]
        out_shape = [jax.ShapeDtypeStruct((batch, dilation, length, width), BF16)]
    else:
        step = DILATION_STEP
        shape = (batch, step, dilation, length // step, width)
        out_blk = pl.BlockSpec((None, step, None, rows // step, width), lambda b, r, t: (b, 0, r, t, 0))
        out_specs = [out_blk, out_blk]
        out_shape = [jax.ShapeDtypeStruct(shape, BF16), jax.ShapeDtypeStruct(shape, F32)]
        scratch += [pltpu.VMEM((ATTN_UNITS_PER_GROUP, blk, V7X_LANES), F32)] * 2
    return pl.pallas_call(
        functools.partial(_attn_body, has_prev=prev is not None, scatter=not last, rows=rows),
        grid=(batch, dilation, length // rows),
        in_specs=in_specs,
        out_specs=out_specs,
        out_shape=out_shape,
        scratch_shapes=scratch,
        compiler_params=_cparams(("parallel", "parallel", "arbitrary")),
        name=f"dilated_attention_d{dilation}",
    )(*args)


def _t5_bucket(distance):
    max_exact = REL_BUCKETS // 2
    scaled = jnp.log(jnp.maximum(distance, max_exact).astype(F32) / max_exact) / math.log(
        REL_MAX_DISTANCE / max_exact)
    large = jnp.minimum(max_exact + (scaled * (REL_BUCKETS - max_exact)).astype(jnp.int32), REL_BUCKETS - 1)
    return jnp.where(distance < max_exact, distance, large)


def _band_bias(rel_bias, window, dilation):
    blk = ATTN_BLOCK
    n_back = window // dilation
    qi = jnp.arange(blk)[:, None]
    kj = jnp.arange(2 * blk)[None, :]
    delta = blk + qi - kj
    band = (delta >= 0) & (delta <= n_back)
    onehot = jax.nn.one_hot(_t5_bucket(jnp.maximum(delta, 0) * dilation), REL_BUCKETS, dtype=F32)
    bias = jnp.einsum("qkb,bh->hqk", onehot, rel_bias.astype(F32) * LOG2E, precision=lax.Precision.HIGHEST)
    bias = jnp.where(band[None], bias, MASKED)
    return jnp.stack([bias, jnp.where(kj[None] >= blk, bias, MASKED)])


def _outproj_even_body(g1_ref, u_ref, z2_ref, oa_ref, uh_ref, cw_ref, w_ref, x_ref, o_ref, slab_ref, *,
                       tiles_per_seq):
    i = pl.program_id(0)
    tm = u_ref.shape[0]
    sub = V7X_SUBLANES
    half = CONV_WIDTH
    dil = oa_ref.shape[0]
    kc = 2 * V7X_LANES
    acc = x_ref[...]
    for c0 in range(0, half, kc):
        for s in range(c0 // V7X_LANES, (c0 + kc) // V7X_LANES):
            for r in range(dil):
                slab_ref[s, pl.ds(r, tm // dil, stride=dil), :] = oa_ref[
                    r, :, s * V7X_LANES:(s + 1) * V7X_LANES].astype(F32)
        attn = jnp.concatenate([slab_ref[s] for s in range(c0 // V7X_LANES, (c0 + kc) // V7X_LANES)], axis=1)
        yb = attn.astype(BF16) * z2_ref[:, c0:c0 + kc]
        acc += jnp.dot(yb, w_ref[half + c0:half + c0 + kc, :], preferred_element_type=F32)

    first = (i % tiles_per_seq) == 0
    row = lax.broadcasted_iota(jnp.int32, (sub, kc), 0)
    for c0 in range(0, half, kc):
        cols = slice(c0, c0 + kc)
        u = u_ref[:, cols].astype(F32)
        halo = jnp.where(first, 0.0, uh_ref[:, cols].astype(F32))
        r1 = pltpu.roll(u, 1, axis=0)
        r2 = pltpu.roll(u, 2, axis=0)
        u1 = jnp.concatenate([jnp.where(row >= 1, r1[0:sub], pltpu.roll(halo, 1, axis=0)), r1[sub:]], axis=0)
        u2 = jnp.concatenate([jnp.where(row >= 2, r2[0:sub], pltpu.roll(halo, 2, axis=0)), r2[sub:]], axis=0)
        conv = cw_ref[0:1, cols] * u2 + cw_ref[1:2, cols] * u1 + cw_ref[2:3, cols] * u
        ya = (g1_ref[:, cols].astype(F32) * conv).astype(BF16)
        acc += jnp.dot(ya, w_ref[cols, :], preferred_element_type=F32)
    o_ref[...] = acc


def _outproj_even(proj, o_attn, conv_w, w_bf16, x2d, seq):
    t, d = x2d.shape
    tm = PROJ_TM
    w = CONV_WIDTH
    halo_blocks = tm // V7X_SUBLANES
    tiles_per_seq = seq // tm
    dil = o_attn.shape[1]

    def colblk(c):
        return pl.BlockSpec((tm, w), lambda i: (i, c))

    return pl.pallas_call(
        functools.partial(_outproj_even_body, tiles_per_seq=tiles_per_seq),
        grid=(t // tm,),
        in_specs=[colblk(G1_TILE), colblk(U_TILE), colblk(Z2_TILE),
                  pl.BlockSpec((None, dil, tm // dil, w), lambda i: (i // tiles_per_seq, 0, i % tiles_per_seq, 0)),
                  pl.BlockSpec((V7X_SUBLANES, w), lambda i: (jnp.maximum(i * halo_blocks - 1, 0), U_TILE)),
                  pl.BlockSpec(conv_w.shape, lambda i: (0, 0)),
                  pl.BlockSpec(w_bf16.shape, lambda i: (0, 0)),
                  pl.BlockSpec((tm, d), lambda i: (i, 0))],
        out_specs=pl.BlockSpec((tm, d), lambda i: (i, 0)),
        out_shape=jax.ShapeDtypeStruct((t, d), F32),
        scratch_shapes=[pltpu.VMEM((N_SLABS, tm, V7X_LANES), F32)],
        compiler_params=_cparams(("parallel",)),
        name="conv_gate_outproj",
    )(proj, proj, proj, o_attn, proj, conv_w.astype(F32), w_bf16, x2d)


def _hgrn_body(q_ref, f_ref, i_ref, z_ref, og_ref, y_ref,
               st_ref, st0_ref, qrows_ref, frows_ref, krows_ref, *, rows):
    c = HGRN_CHUNK
    hd = HGRN_DIM
    heads = q_ref.shape[2] // hd
    chunks = rows // c
    t = pl.program_id(2)

    @pl.when(t == 0)
    def _():
        st_ref[...] = jnp.zeros_like(st_ref)

    st0_ref[...] = st_ref[...]
    ri = lax.broadcasted_iota(jnp.int32, (c, c), 0)
    ci = lax.broadcasted_iota(jnp.int32, (c, c), 1)
    causal = ri >= ci

    def lanes(h):
        return slice(h * hd, (h + 1) * hd)

    def finish(o, sl, h):
        ms = jnp.mean(o * o, axis=-1, keepdims=True)
        y = o * lax.rsqrt(ms + EPS) * og_ref[:, lanes(h)] * z_ref[0, sl, lanes(h)].astype(F32)
        y_ref[0, sl, lanes(h)] = y.astype(y_ref.dtype)

    worst = [jnp.zeros((1, hd), F32)]
    pre = {}
    tril = jnp.where(causal, 1.0, 0.0).astype(BF16)

    def stage_prefix(units):
        for h, n in units:
            sl = slice(n * c, (n + 1) * c)
            pre[h, n] = dict(gcum=jnp.dot(tril, f_ref[0, sl, lanes(h)], preferred_element_type=F32))

    def stage_scale(units):
        for h, n in units:
            sl = slice(n * c, (n + 1) * c)
            d = pre[h, n]
            gcum = d["gcum"]
            q = q_ref[0, sl, lanes(h)].astype(F32)
            k = 1.0 - jnp.exp2(f_ref[0, sl, lanes(h)].astype(F32))
            rho = 0.5 * gcum[c - 1:c]
            worst[0] = jnp.minimum(worst[0], jnp.where(rho == rho, rho, -jnp.inf))
            d.update(qt=(q * jnp.exp2(gcum - rho)).astype(BF16), kh=(k * jnp.exp2(rho - gcum)).astype(BF16),
                     e_rho=jnp.exp2(rho), v=i_ref[0, sl, lanes(h)])

    def stage_state(units):
        for u in units:
            pre[u]["dst"] = lax.dot_general(pre[u]["v"], pre[u]["kh"], (((0,), (0,)), ((), ())),
                                            preferred_element_type=F32) * pre[u]["e_rho"]
        for h in sorted({h for h, _ in units}):
            st = st_ref[h]
            for n in range(chunks):
                d = pre[h, n]
                d["rhs"] = jnp.concatenate([d["kh"], (st * d["e_rho"]).astype(BF16)], axis=0)
                st = st * (d["e_rho"] * d["e_rho"]) + d["dst"]
            st_ref[h] = st

    def stage_scores(units):
        for u in units:
            pre[u]["so"] = lax.dot_general(pre[u]["qt"], pre[u]["rhs"], (((1,), (1,)), ((), ())),
                                           preferred_element_type=F32)

    def stage_output(units):
        for u in units:
            scores = jnp.where(causal, pre[u]["so"][:, :c], 0.0).astype(BF16)
            pre[u]["o"] = pre[u]["so"][:, c:] + jnp.dot(scores, pre[u]["v"], preferred_element_type=F32)
        for h, n in units:
            finish(pre[h, n]["o"], slice(n * c, (n + 1) * c), h)

    gsize = HGRN_HEADS_PER_GROUP
    groups = [[(h, n) for h in range(g, g + gsize) for n in range(chunks)] for g in range(0, heads, gsize)]
    stages = (stage_prefix, stage_scale, stage_state, stage_scores, stage_output)
    for tick in range(len(stages) + len(groups) - 1):
        for gi, units in enumerate(groups):
            if 0 <= tick - gi < len(stages):
                stages[tick - gi](units)
    worst = worst[0]

    @pl.when(jnp.min(worst) < -HGRN_MAX_HALF_DECAY)
    def _():
        lane = lax.broadcasted_iota(jnp.int32, (hd, c), 1)
        for h in range(heads):
            for n in range(chunks):
                sl = slice(n * c, (n + 1) * c)
                decay = jnp.exp2(f_ref[0, sl, lanes(h)].astype(F32))
                qrows_ref[...] = q_ref[0, sl, lanes(h)].astype(F32)
                frows_ref[...] = decay
                krows_ref[...] = 1.0 - decay
                vt = jnp.transpose(i_ref[0, sl, lanes(h)].astype(F32))

                def step(s, carry):
                    st, ot = carry
                    vcol = jnp.sum(jnp.where(lane == s, vt, 0.0), axis=-1, keepdims=True)
                    st = st * frows_ref[pl.ds(s, 1), :] + vcol * krows_ref[pl.ds(s, 1), :]
                    ocol = jnp.sum(st * qrows_ref[pl.ds(s, 1), :], axis=-1, keepdims=True)
                    ot = jnp.where(lane == s, ocol, ot)
                    return st, ot

                st, ot = lax.fori_loop(0, c, step, (st0_ref[h], jnp.zeros((hd, c), F32)))
                st0_ref[h] = st
                finish(jnp.transpose(ot), sl, h)
        st_ref[...] = st0_ref[...]


def _hgrn2(proj, o_gain, batch, seq):
    rows = HGRN_ROWS
    hd = HGRN_DIM
    heads = HGRN_HEADS_PER_STEP
    width = heads * hd
    projv = proj.reshape(batch, seq, IN_WIDTH)
    per_kind = HGRN_HEADS // heads

    def part(kind):
        return pl.BlockSpec((1, rows, width), lambda b, h, t: (b, t, kind * per_kind + h))

    def per_head():
        return pl.BlockSpec((1, width), lambda b, h, t: (0, h))

    y = pl.pallas_call(
        functools.partial(_hgrn_body, rows=rows),
        grid=(batch, per_kind, seq // rows),
        in_specs=[part(0), part(1), part(2), part(3), per_head()],
        out_specs=pl.BlockSpec((1, rows, width), lambda b, h, t: (b, t, h)),
        out_shape=jax.ShapeDtypeStruct((batch, seq, MIX_WIDTH), BF16),
        scratch_shapes=[pltpu.VMEM((heads, hd, hd), F32), pltpu.VMEM((heads, hd, hd), F32)]
        + [pltpu.VMEM((HGRN_CHUNK, hd), F32)] * 3,
        compiler_params=_cparams(("parallel", "parallel", "arbitrary")),
        name="hgrn2_recurrence",
    )(projv, projv, projv, projv, o_gain.reshape(1, -1).astype(F32))
    return y.reshape(batch * seq, MIX_WIDTH)


def _outproj_body(y_ref, w_ref, x_ref, o_ref):
    o_ref[...] = x_ref[...] + jnp.dot(y_ref[...], w_ref[...], preferred_element_type=F32)


def _outproj(y, w_bf16, x2d):
    t, d = x2d.shape
    tm = PROJ_TM
    return pl.pallas_call(
        _outproj_body,
        grid=(t // tm,),
        in_specs=[pl.BlockSpec((tm, y.shape[1]), lambda i: (i, 0)),
                  pl.BlockSpec(w_bf16.shape, lambda i: (0, 0)),
                  pl.BlockSpec((tm, d), lambda i: (i, 0))],
        out_specs=pl.BlockSpec((tm, d), lambda i: (i, 0)),
        out_shape=jax.ShapeDtypeStruct((t, d), F32),
        compiler_params=_cparams(("parallel",)),
        name="outproj_residual",
    )(y, w_bf16, x2d)


def kernel(x, ln_even, w_in_even, conv_w, q_gain, k_gain, w_out_even, rel_bias, ln_odd, w_in_odd,
           lower_bounds, o_gain, w_out_odd):
    batch, seq, d = x.shape
    assert d == D_MODEL and seq % (DILATED_PATTERNS[-1][1] * ATTN_BLOCK) == 0 and seq % PROJ_TM == 0
    depth = ln_even.shape[0] + ln_odd.shape[0]
    x2d = x.reshape(batch * seq, d).astype(F32)

    lbs = jnp.cumsum(jax.nn.softmax(lower_bounds.astype(F32), axis=0), axis=0)
    lbs = lbs - lbs[0:1]
    biases = [_band_bias(rel_bias, window, dilation) for window, dilation in DILATED_PATTERNS]
    ones = jnp.ones((1, IN_WIDTH), F32)
    q_tile, k_tile = EVEN_PLAN[0][1], EVEN_PLAN[1][1]
    w_in_even, w_in_odd, w_out_even, w_out_odd = map(_to_bf16, (w_in_even, w_in_odd, w_out_even, w_out_odd))

    for layer in range(depth):
        j = layer // 2
        if layer % 2 == 0:
            q_scale = jnp.tile(q_gain[j].astype(F32), ATTN_HEADS) * (ATTN_HEAD_DIM ** -0.5 * LOG2E)
            colscale = ones.at[0, q_tile * PROJ_TN:(q_tile + 1) * PROJ_TN].set(q_scale)
            colscale = colscale.at[0, k_tile * PROJ_TN:(k_tile + 1) * PROJ_TN].set(
                jnp.tile(k_gain[j].astype(F32), ATTN_HEADS))
            proj, qkv4, qkv16 = _norm_inproj(x2d, ln_even[j], w_in_even[j], colscale, ones,
                                             EVEN_PLAN, False, batch, seq)
            prev = None
            for index, srcs in enumerate((proj, qkv4, qkv16)):
                prev = _attention_pattern(index, srcs, biases[index], prev, batch, seq)
            x2d = _outproj_even(proj, prev[0], conv_w[j], w_out_even[j], x2d, seq)
        else:
            gate_tiles = [jt for kind, jt, _ in ODD_PLAN if kind == "log2gate"]
            gate_cols = slice(min(gate_tiles) * PROJ_TN, (max(gate_tiles) + 1) * PROJ_TN)
            gate_mid = ones.at[0, gate_cols].set(0.5 * (1.0 + lbs[j]))
            gate_half_range = ones.at[0, gate_cols].set(0.5 * (1.0 - lbs[j]))
            (proj,) = _norm_inproj(x2d, ln_odd[j], w_in_odd[j], gate_mid, gate_half_range,
                                   ODD_PLAN, j == 0, batch, seq)
            y = _hgrn2(proj, o_gain[j], batch, seq)
            x2d = _outproj(y, w_out_odd[j], x2d)
    return x2d.reshape(batch, seq, d).astype(x.dtype)
```

```python
import functools
import math

import jax
import jax.numpy as jnp
import numpy as np
from jax import lax
from jax.experimental import pallas as pl
from jax.experimental.pallas import tpu as pltpu

F32 = jnp.float32
BF16 = jnp.bfloat16

D_MODEL = 1024
MIX_WIDTH = 2 * D_MODEL
CONV_WIDTH = MIX_WIDTH // 2
ATTN_HEAD_DIM = 64
ATTN_HEADS = 16
ATTN_WIDTH = ATTN_HEADS * ATTN_HEAD_DIM
ATTN_BLOCK = 128
DILATED_PATTERNS = ((128, 1), (512, 4), (2048, 16))
DILATION_STEP = 4
REL_BUCKETS = 32
REL_MAX_DISTANCE = 2048
HGRN_HEADS = 16
HGRN_DIM = 128
IN_WIDTH = 8192
EPS = 1e-6
MASKED = -1e30
LOG2E = math.log2(math.e)

V7X_LANES = 128
V7X_SUBLANES = 8
V7X_VMEM_LIMIT_BYTES = 56 * 1024 * 1024

PROJ_TM = 512
INPROJ_TM_EVEN = 256
INPROJ_TM_ODD = 512
PROJ_TN = 1024
ATTN_ROWS = 512
ATTN_UNITS_PER_GROUP = 4
HGRN_ROWS = 512
HGRN_CHUNK = 128
HGRN_HEADS_PER_STEP = 4
HGRN_HEADS_PER_GROUP = 2
HGRN_MAX_HALF_DECAY = 100.0

N_SLABS = ATTN_WIDTH // V7X_LANES


def _cparams(semantics):
    return pltpu.CompilerParams(dimension_semantics=semantics, vmem_limit_bytes=V7X_VMEM_LIMIT_BYTES)


def _silu(t):
    half = 0.5 * t
    return half + half * jnp.tanh(half)


EVEN_PLAN = (("headnorm", 3, 0), ("headnorm", 4, 1), ("scatter", 5, 2),
             ("gate_product", (0, 6), 3),
             ("product", (1, 2), 4),
             ("silu", 7, 5))
ODD_PLAN = (("silu", 0, 0), ("silu", 1, 1), ("log2gate", 2, 2), ("log2gate", 3, 3),
            ("plain", 4, 4), ("plain", 5, 5), ("silu", 6, 6), ("silu", 7, 7))
QKV_OUT_TILES = 3
EVEN_OUT_WIDTH = len(EVEN_PLAN) * PROJ_TN
G1_TILE, U_TILE, Z2_TILE = 3, 4, 5


def _log2_forget_gate(f_pre, lb_terms):
    if lb_terms is None:
        log_f = jnp.minimum(f_pre, 0.0) - jnp.log(1.0 + jnp.exp(-jnp.abs(f_pre)))
    else:
        mid, half_range = lb_terms
        log_f = jnp.log(mid + half_range * jnp.tanh(0.5 * f_pre))
    return log_f * LOG2E


def _residue_major_permutation(rows, dilation):
    out = np.arange(rows)
    src = dilation * (out % (rows // dilation)) + out // (rows // dilation)
    return jnp.asarray(src[:, None] == np.arange(rows)[None, :], BF16)


def _inproj_body(x_ref, g_ref, w_ref, ca_ref, cb_ref, bd_ref, *rest, plan, zero_lb):
    scatter = any(kind == "scatter" for kind, _, _ in plan)
    if scatter:
        p4_ref, p16_ref, o_ref, o4_ref, o16_ref = rest
    else:
        (o_ref,) = rest
    x = x_ref[...]
    ms = jnp.mean(x * x, axis=-1, keepdims=True)
    hn = (x * lax.rsqrt(ms + EPS) * g_ref[...]).astype(BF16)
    tm = x.shape[0]
    tn = PROJ_TN

    def project(jt):
        return jnp.dot(hn, w_ref[:, jt * tn:(jt + 1) * tn], preferred_element_type=F32)

    def emit_residue_major(res, cols, part):
        res = res.astype(BF16)
        o_ref[:, cols] = res
        for perm_ref, ref in ((p4_ref, o4_ref), (p16_ref, o16_ref)):
            dil = ref.shape[0]
            moved = jnp.dot(perm_ref[...], res, preferred_element_type=F32).astype(ref.dtype)
            for r in range(dil):
                ref[r, :, part * tn:(part + 1) * tn] = moved[r * (tm // dil):(r + 1) * (tm // dil)]

    for kind, jt, out_tile in plan:
        cols = slice(out_tile * tn, (out_tile + 1) * tn)
        if kind == "product":
            o_ref[:, cols] = (project(jt[0]) * project(jt[1])).astype(o_ref.dtype)
        elif kind == "gate_product":
            o_ref[:, cols] = (project(jt[0]) * _silu(project(jt[1]))).astype(o_ref.dtype)
        elif kind == "plain":
            o_ref[:, cols] = project(jt).astype(o_ref.dtype)
        elif kind == "silu":
            o_ref[:, cols] = _silu(project(jt)).astype(o_ref.dtype)
        elif kind == "log2gate":
            wcols = slice(jt * tn, (jt + 1) * tn)
            lb_terms = None if zero_lb else (ca_ref[:, wcols], cb_ref[:, wcols])
            o_ref[:, cols] = _log2_forget_gate(project(jt), lb_terms).astype(o_ref.dtype)
        elif kind == "scatter":
            emit_residue_major(project(jt), cols, out_tile)
        else:
            acc = project(jt)
            width = bd_ref.shape[0]
            parts = []
            for s in range(tn // width):
                a = acc[:, s * width:(s + 1) * width]
                ms = jnp.dot((a * a).astype(BF16), bd_ref[...], preferred_element_type=F32)
                parts.append(a * lax.rsqrt(ms + EPS) * ca_ref[:, jt * tn + s * width:jt * tn + (s + 1) * width])
            emit_residue_major(jnp.concatenate(parts, axis=1), cols, out_tile)


def _norm_inproj(x2d, gain, w_bf16, layer, col_a, col_b, plan, zero_lb, batch, seq):
    t, d = x2d.shape
    n = w_bf16.shape[2]
    scatter = any(kind == "scatter" for kind, _, _ in plan)
    tm, tn = (INPROJ_TM_EVEN if scatter else INPROJ_TM_ODD), PROJ_TN
    n_out = len(plan) * tn
    width = 2 * V7X_LANES
    head_of_lane = np.arange(width) // ATTN_HEAD_DIM
    bd = jnp.asarray((head_of_lane[:, None] == head_of_lane[None, :]) / ATTN_HEAD_DIM, BF16)
    tiles_per_seq = seq // tm
    out_specs = [pl.BlockSpec((tm, n_out), lambda i: (i, 0))]
    out_shape = [jax.ShapeDtypeStruct((t, n_out), BF16)]
    perms = []
    if scatter:
        qkv_width = QKV_OUT_TILES * tn
        for _, dil in DILATED_PATTERNS[1:]:
            out_specs.append(pl.BlockSpec((None, dil, tm // dil, qkv_width),
                                          lambda i: (i // tiles_per_seq, 0, i % tiles_per_seq, 0)))
            out_shape.append(jax.ShapeDtypeStruct((batch, dil, seq // dil, qkv_width), BF16))
            perms.append(_residue_major_permutation(tm, dil))
    resident = dict(pipeline_mode=pl.Buffered(1))
    return pl.pallas_call(
        functools.partial(_inproj_body, plan=plan, zero_lb=zero_lb),
        grid=(t // tm,),
        in_specs=[
            pl.BlockSpec((tm, d), lambda i: (i, 0)),
            pl.BlockSpec((1, d), lambda i: (0, 0)),
            pl.BlockSpec((None, d, n), lambda i: (layer, 0, 0), **resident),
            pl.BlockSpec((1, n), lambda i: (0, 0)),
            pl.BlockSpec((1, n), lambda i: (0, 0)),
            pl.BlockSpec((width, width), lambda i: (0, 0)),
        ] + [pl.BlockSpec((tm, tm), lambda i: (0, 0))] * len(perms),
        out_specs=out_specs,
        out_shape=out_shape,
        compiler_params=_cparams(("parallel",)),
        name="norm_inproj_even" if scatter else "norm_inproj_odd",
    )(x2d, gain.reshape(1, d).astype(F32), w_bf16, col_a, col_b, bd, *perms)


def _attn_body(*refs, has_prev, scatter, rows):
    refs = list(refs)
    qkv_ref, pqkv_ref, bias_ref = refs[:3]
    del refs[:3]
    if has_prev:
        po_ref, pl_ref = refs[:2]
        del refs[:2]
    if scatter:
        o_ref, l_ref, so_ref, sl_ref = refs
    else:
        (o_ref,) = refs
    blk = ATTN_BLOCK
    width = ATTN_WIDTH
    t = pl.program_id(2)

    def window(part, qb, c0):
        cols = slice(part * width + c0, part * width + c0 + V7X_LANES)
        if qb == 0:
            return jnp.concatenate([pqkv_ref[:, cols], qkv_ref[0:blk, cols]], axis=0)
        return qkv_ref[(qb - 1) * blk:(qb + 1) * blk, cols]

    lane = lax.broadcasted_iota(jnp.int32, (blk, V7X_LANES), 1)
    low = lane < ATTN_HEAD_DIM
    low_bf = jnp.where(low, 1.0, 0.0).astype(BF16)
    high_bf = jnp.where(low, 0.0, 1.0).astype(BF16)
    low2_bf = jnp.concatenate([low_bf, low_bf], axis=0)
    high2_bf = jnp.concatenate([high_bf, high_bf], axis=0)
    n_pairs = ATTN_HEADS // 2
    step = DILATION_STEP
    sub = blk // step

    def locate(idx):
        qb = idx // n_pairs
        hp = idx % n_pairs
        return qb, hp, qb * blk, hp * V7X_LANES

    def logits(idx):
        qb, hp, r0, c0 = locate(idx)
        q2 = qkv_ref[r0:r0 + blk, c0:c0 + V7X_LANES]
        kw = window(1, qb, c0)
        qs = jnp.concatenate([q2 * low_bf, q2 * high_bf], axis=0)
        return lax.dot_general(qs, kw, (((1,), (1,)), ((), ())), preferred_element_type=F32)

    def softmax(idx, s):
        qb, hp, _, _ = locate(idx)
        first = (t == 0).astype(jnp.int32) if qb == 0 else 0
        s = s + jnp.concatenate([bias_ref[first, 2 * hp], bias_ref[first, 2 * hp + 1]], axis=0)
        m = jnp.max(s, axis=-1, keepdims=True)
        p = jnp.exp2(s - m).astype(BF16)
        return p, jnp.where(low, m[0:blk], m[blk:2 * blk])

    def weighted(idx, p):
        qb, _, _, c0 = locate(idx)
        vw = window(2, qb, c0)
        rhs = jnp.concatenate([jnp.concatenate([vw * low2_bf, low2_bf], axis=1),
                               jnp.concatenate([vw * high2_bf, high2_bf], axis=1)], axis=0)
        lhs = jnp.concatenate([p[0:blk], p[blk:2 * blk]], axis=1)
        return jnp.dot(lhs, rhs, preferred_element_type=F32)

    def finish(idx, slot, pv, m2):
        qb, _, r0, c0 = locate(idx)
        den = pv[:, V7X_LANES:]
        o2 = pv[:, 0:V7X_LANES] / den
        if has_prev or scatter:
            lse2 = m2 + jnp.log(den) * LOG2E
        if has_prev:
            po = po_ref[pl.ds(r0, blk), pl.ds(c0, V7X_LANES)].astype(F32)
            plse = pl_ref[pl.ds(r0, blk), pl.ds(c0, V7X_LANES)]
            gap = plse - lse2
            e = jnp.exp2(-jnp.abs(gap))
            prev_larger = gap >= 0.0
            tot = 1.0 + e
            o2 = (jnp.where(prev_larger, po, o2) + e * jnp.where(prev_larger, o2, po)) / tot
            lse2 = jnp.maximum(plse, lse2) + jnp.log(tot) * LOG2E
        if not scatter:
            o_ref[pl.ds(r0, blk), pl.ds(c0, V7X_LANES)] = o2.astype(o_ref.dtype)
            return
        so_ref[slot] = o2
        sl_ref[slot] = lse2
        d0 = qb * sub
        for jm in range(step):
            o_ref[jm, pl.ds(d0, sub), pl.ds(c0, V7X_LANES)] = so_ref[
                slot, pl.ds(jm, sub, stride=step), :].astype(o_ref.dtype)
            l_ref[jm, pl.ds(d0, sub), pl.ds(c0, V7X_LANES)] = sl_ref[slot, pl.ds(jm, sub, stride=step), :]

    per_group = ATTN_UNITS_PER_GROUP
    n_units = (rows // blk) * n_pairs
    groups = [list(range(g, g + per_group)) for g in range(0, n_units, per_group)]
    pending = None
    for units in groups + [None]:
        pvs = None
        if pending is not None:
            prev_units, probs = pending
            pvs = [weighted(u, p) for u, (p, _) in zip(prev_units, probs)]
        scores = [logits(u) for u in units] if units is not None else None
        if pvs is not None:
            for slot, (u, pv, (_, m2)) in enumerate(zip(prev_units, pvs, probs)):
                finish(u, slot, pv, m2)
        pending = (units, [softmax(u, s) for u, s in zip(units, scores)]) if units is not None else None


def _attention_pattern(index, srcs, bias, prev, batch, seq):
    dilation = DILATED_PATTERNS[index][1]
    last = index == len(DILATED_PATTERNS) - 1
    length = seq // dilation
    rows = min(ATTN_ROWS, length)
    blk = ATTN_BLOCK
    width = ATTN_WIDTH
    per_step = rows // blk

    qkv_width = QKV_OUT_TILES * width
    if index == 0:
        src = srcs.reshape(batch, seq, EVEN_OUT_WIDTH)
        cur = pl.BlockSpec((None, rows, qkv_width), lambda b, r, t: (b, t, 0))
        prv = pl.BlockSpec((None, blk, qkv_width), lambda b, r, t: (b, jnp.maximum(t * per_step - 1, 0), 0))
    else:
        src = srcs
        cur = pl.BlockSpec((None, None, rows, qkv_width), lambda b, r, t: (b, r, t, 0))
        prv = pl.BlockSpec((None, None, blk, qkv_width),
                           lambda b, r, t: (b, r, jnp.maximum(t * per_step - 1, 0), 0))

    res_blk = pl.BlockSpec((None, None, rows, width), lambda b, r, t: (b, r, t, 0))
    in_specs = [cur, prv, pl.BlockSpec(bias.shape, lambda b, r, t: (0, 0, 0, 0))]
    args = [src, src, bias]
    if prev is not None:
        in_specs += [res_blk, res_blk]
        args += [prev[0].reshape(batch, dilation, length, width), prev[1].reshape(batch, dilation, length, width)]
    scratch = []
    if last:
        out_specs = [res_blk]
        out_shape = [jax.ShapeDtypeStruct((batch, dilation, length, width), BF16)]
    else:
        step = DILATION_STEP
        shape = (batch, step, dilation, length // step, width)
        out_blk = pl.BlockSpec((None, step, None, rows // step, width), lambda b, r, t: (b, 0, r, t, 0))
        out_specs = [out_blk, out_blk]
        out_shape = [jax.ShapeDtypeStruct(shape, BF16), jax.ShapeDtypeStruct(shape, F32)]
        scratch += [pltpu.VMEM((ATTN_UNITS_PER_GROUP, blk, V7X_LANES), F32)] * 2
    return pl.pallas_call(
        functools.partial(_attn_body, has_prev=prev is not None, scatter=not last, rows=rows),
        grid=(batch, dilation, length // rows),
        in_specs=in_specs,
        out_specs=out_specs,
        out_shape=out_shape,
        scratch_shapes=scratch,
        compiler_params=_cparams(("parallel", "parallel", "arbitrary")),
        name=f"dilated_attention_d{dilation}",
    )(*args)


def _t5_bucket(distance):
    max_exact = REL_BUCKETS // 2
    scaled = jnp.log(jnp.maximum(distance, max_exact).astype(F32) / max_exact) / math.log(
        REL_MAX_DISTANCE / max_exact)
    large = jnp.minimum(max_exact + (scaled * (REL_BUCKETS - max_exact)).astype(jnp.int32), REL_BUCKETS - 1)
    return jnp.where(distance < max_exact, distance, large)


def _band_bias(rel_bias, window, dilation):
    blk = ATTN_BLOCK
    n_back = window // dilation
    qi = jnp.arange(blk)[:, None]
    kj = jnp.arange(2 * blk)[None, :]
    delta = blk + qi - kj
    band = (delta >= 0) & (delta <= n_back)
    onehot = jax.nn.one_hot(_t5_bucket(jnp.maximum(delta, 0) * dilation), REL_BUCKETS, dtype=F32)
    bias = jnp.einsum("qkb,bh->hqk", onehot, rel_bias.astype(F32) * LOG2E, precision=lax.Precision.HIGHEST)
    bias = jnp.where(band[None], bias, MASKED)
    return jnp.stack([bias, jnp.where(kj[None] >= blk, bias, MASKED)])


def _outproj_even_body(g1_ref, u_ref, z2_ref, oa_ref, uh_ref, cw_ref, w_ref, x_ref, o_ref, slab_ref, *,
                       tiles_per_seq):
    i = pl.program_id(0)
    tm = u_ref.shape[0]
    sub = V7X_SUBLANES
    half = CONV_WIDTH
    dil = oa_ref.shape[0]
    kc = 2 * V7X_LANES
    acc = x_ref[...]
    for c0 in range(0, half, kc):
        for s in range(c0 // V7X_LANES, (c0 + kc) // V7X_LANES):
            for r in range(dil):
                slab_ref[s, pl.ds(r, tm // dil, stride=dil), :] = oa_ref[
                    r, :, s * V7X_LANES:(s + 1) * V7X_LANES].astype(F32)
        attn = jnp.concatenate([slab_ref[s] for s in range(c0 // V7X_LANES, (c0 + kc) // V7X_LANES)], axis=1)
        yb = attn.astype(BF16) * z2_ref[:, c0:c0 + kc]
        acc += jnp.dot(yb, w_ref[half + c0:half + c0 + kc, :], preferred_element_type=F32)

    first = (i % tiles_per_seq) == 0
    row = lax.broadcasted_iota(jnp.int32, (sub, kc), 0)
    for c0 in range(0, half, kc):
        cols = slice(c0, c0 + kc)
        u = u_ref[:, cols].astype(F32)
        halo = jnp.where(first, 0.0, uh_ref[:, cols].astype(F32))
        r1 = pltpu.roll(u, 1, axis=0)
        r2 = pltpu.roll(u, 2, axis=0)
        u1 = jnp.concatenate([jnp.where(row >= 1, r1[0:sub], pltpu.roll(halo, 1, axis=0)), r1[sub:]], axis=0)
        u2 = jnp.concatenate([jnp.where(row >= 2, r2[0:sub], pltpu.roll(halo, 2, axis=0)), r2[sub:]], axis=0)
        conv = cw_ref[0:1, cols] * u2 + cw_ref[1:2, cols] * u1 + cw_ref[2:3, cols] * u
        ya = (g1_ref[:, cols].astype(F32) * conv).astype(BF16)
        acc += jnp.dot(ya, w_ref[cols, :], preferred_element_type=F32)
    o_ref[...] = acc


def _outproj_even(proj, o_attn, conv_w, w_bf16, layer, x2d, seq):
    t, d = x2d.shape
    tm = PROJ_TM
    w = CONV_WIDTH
    halo_blocks = tm // V7X_SUBLANES
    tiles_per_seq = seq // tm
    dil = o_attn.shape[1]

    def colblk(c):
        return pl.BlockSpec((tm, w), lambda i: (i, c))

    return pl.pallas_call(
        functools.partial(_outproj_even_body, tiles_per_seq=tiles_per_seq),
        grid=(t // tm,),
        in_specs=[colblk(G1_TILE), colblk(U_TILE), colblk(Z2_TILE),
                  pl.BlockSpec((None, dil, tm // dil, w), lambda i: (i // tiles_per_seq, 0, i % tiles_per_seq, 0)),
                  pl.BlockSpec((V7X_SUBLANES, w), lambda i: (jnp.maximum(i * halo_blocks - 1, 0), U_TILE)),
                  pl.BlockSpec(conv_w.shape, lambda i: (0, 0)),
                  pl.BlockSpec((None,) + w_bf16.shape[1:], lambda i: (layer, 0, 0)),
                  pl.BlockSpec((tm, d), lambda i: (i, 0))],
        out_specs=pl.BlockSpec((tm, d), lambda i: (i, 0)),
        out_shape=jax.ShapeDtypeStruct((t, d), F32),
        scratch_shapes=[pltpu.VMEM((N_SLABS, tm, V7X_LANES), F32)],
        compiler_params=_cparams(("parallel",)),
        name="conv_gate_outproj",
    )(proj, proj, proj, o_attn, proj, conv_w.astype(F32), w_bf16, x2d)


def _hgrn_body(q_ref, f_ref, i_ref, z_ref, og_ref, y_ref,
               st_ref, st0_ref, qrows_ref, frows_ref, krows_ref, *, rows):
    c = HGRN_CHUNK
    hd = HGRN_DIM
    heads = q_ref.shape[2] // hd
    chunks = rows // c
    t = pl.program_id(2)

    @pl.when(t == 0)
    def _():
        st_ref[...] = jnp.zeros_like(st_ref)

    st0_ref[...] = st_ref[...]
    ri = lax.broadcasted_iota(jnp.int32, (c, c), 0)
    ci = lax.broadcasted_iota(jnp.int32, (c, c), 1)
    causal = ri >= ci

    def lanes(h):
        return slice(h * hd, (h + 1) * hd)

    def finish(o, sl, h):
        ms = jnp.mean(o * o, axis=-1, keepdims=True)
        y = o * lax.rsqrt(ms + EPS) * og_ref[:, lanes(h)] * z_ref[0, sl, lanes(h)].astype(F32)
        y_ref[0, sl, lanes(h)] = y.astype(y_ref.dtype)

    worst = [jnp.zeros((1, hd), F32)]
    pre = {}
    tril = jnp.where(causal, 1.0, 0.0).astype(BF16)

    def stage_prefix(units):
        for h, n in units:
            sl = slice(n * c, (n + 1) * c)
            pre[h, n] = dict(gcum=jnp.dot(tril, f_ref[0, sl, lanes(h)], preferred_element_type=F32))

    def stage_scale(units):
        for h, n in units:
            sl = slice(n * c, (n + 1) * c)
            d = pre[h, n]
            gcum = d["gcum"]
            q = q_ref[0, sl, lanes(h)].astype(F32)
            k = 1.0 - jnp.exp2(f_ref[0, sl, lanes(h)].astype(F32))
            rho = 0.5 * gcum[c - 1:c]
            worst[0] = jnp.minimum(worst[0], jnp.where(rho == rho, rho, -jnp.inf))
            d.update(qt=(q * jnp.exp2(gcum - rho)).astype(BF16), kh=(k * jnp.exp2(rho - gcum)).astype(BF16),
                     e_rho=jnp.exp2(rho), v=i_ref[0, sl, lanes(h)])

    def stage_state(units):
        for u in units:
            pre[u]["dst"] = lax.dot_general(pre[u]["v"], pre[u]["kh"], (((0,), (0,)), ((), ())),
                                            preferred_element_type=F32) * pre[u]["e_rho"]
        for h in sorted({h for h, _ in units}):
            st = st_ref[h]
            for n in range(chunks):
                d = pre[h, n]
                d["rhs"] = jnp.concatenate([d["kh"], (st * d["e_rho"]).astype(BF16)], axis=0)
                st = st * (d["e_rho"] * d["e_rho"]) + d["dst"]
            st_ref[h] = st

    def stage_scores(units):
        for u in units:
            pre[u]["so"] = lax.dot_general(pre[u]["qt"], pre[u]["rhs"], (((1,), (1,)), ((), ())),
                                           preferred_element_type=F32)

    def stage_output(units):
        for u in units:
            scores = jnp.where(causal, pre[u]["so"][:, :c], 0.0).astype(BF16)
            pre[u]["o"] = pre[u]["so"][:, c:] + jnp.dot(scores, pre[u]["v"], preferred_element_type=F32)
        for h, n in units:
            finish(pre[h, n]["o"], slice(n * c, (n + 1) * c), h)

    gsize = HGRN_HEADS_PER_GROUP
    groups = [[(h, n) for h in range(g, g + gsize) for n in range(chunks)] for g in range(0, heads, gsize)]
    stages = (stage_prefix, stage_scale, stage_state, stage_scores, stage_output)
    for tick in range(len(stages) + len(groups) - 1):
        for gi, units in enumerate(groups):
            if 0 <= tick - gi < len(stages):
                stages[tick - gi](units)
    worst = worst[0]

    @pl.when(jnp.min(worst) < -HGRN_MAX_HALF_DECAY)
    def _():
        lane = lax.broadcasted_iota(jnp.int32, (hd, c), 1)
        for h in range(heads):
            for n in range(chunks):
                sl = slice(n * c, (n + 1) * c)
                decay = jnp.exp2(f_ref[0, sl, lanes(h)].astype(F32))
                qrows_ref[...] = q_ref[0, sl, lanes(h)].astype(F32)
                frows_ref[...] = decay
                krows_ref[...] = 1.0 - decay
                vt = jnp.transpose(i_ref[0, sl, lanes(h)].astype(F32))

                def step(s, carry):
                    st, ot = carry
                    vcol = jnp.sum(jnp.where(lane == s, vt, 0.0), axis=-1, keepdims=True)
                    st = st * frows_ref[pl.ds(s, 1), :] + vcol * krows_ref[pl.ds(s, 1), :]
                    ocol = jnp.sum(st * qrows_ref[pl.ds(s, 1), :], axis=-1, keepdims=True)
                    ot = jnp.where(lane == s, ocol, ot)
                    return st, ot

                st, ot = lax.fori_loop(0, c, step, (st0_ref[h], jnp.zeros((hd, c), F32)))
                st0_ref[h] = st
                finish(jnp.transpose(ot), sl, h)
        st_ref[...] = st0_ref[...]


def _hgrn2(proj, o_gain, batch, seq):
    rows = HGRN_ROWS
    hd = HGRN_DIM
    heads = HGRN_HEADS_PER_STEP
    width = heads * hd
    projv = proj.reshape(batch, seq, IN_WIDTH)
    per_kind = HGRN_HEADS // heads

    def part(kind):
        return pl.BlockSpec((1, rows, width), lambda b, h, t: (b, t, kind * per_kind + h))

    def per_head():
        return pl.BlockSpec((1, width), lambda b, h, t: (0, h))

    y = pl.pallas_call(
        functools.partial(_hgrn_body, rows=rows),
        grid=(batch, per_kind, seq // rows),
        in_specs=[part(0), part(1), part(2), part(3), per_head()],
        out_specs=pl.BlockSpec((1, rows, width), lambda b, h, t: (b, t, h)),
        out_shape=jax.ShapeDtypeStruct((batch, seq, MIX_WIDTH), BF16),
        scratch_shapes=[pltpu.VMEM((heads, hd, hd), F32), pltpu.VMEM((heads, hd, hd), F32)]
        + [pltpu.VMEM((HGRN_CHUNK, hd), F32)] * 3,
        compiler_params=_cparams(("parallel", "parallel", "arbitrary")),
        name="hgrn2_recurrence",
    )(projv, projv, projv, projv, o_gain.reshape(1, -1).astype(F32))
    return y.reshape(batch * seq, MIX_WIDTH)


def _outproj_body(y_ref, w_ref, x_ref, o_ref):
    o_ref[...] = x_ref[...] + jnp.dot(y_ref[...], w_ref[...], preferred_element_type=F32)


def _outproj(y, w_bf16, layer, x2d):
    t, d = x2d.shape
    tm = PROJ_TM
    return pl.pallas_call(
        _outproj_body,
        grid=(t // tm,),
        in_specs=[pl.BlockSpec((tm, y.shape[1]), lambda i: (i, 0)),
                  pl.BlockSpec((None,) + w_bf16.shape[1:], lambda i: (layer, 0, 0)),
                  pl.BlockSpec((tm, d), lambda i: (i, 0))],
        out_specs=pl.BlockSpec((tm, d), lambda i: (i, 0)),
        out_shape=jax.ShapeDtypeStruct((t, d), F32),
        compiler_params=_cparams(("parallel",)),
        name="outproj_residual",
    )(y, w_bf16, x2d)


def kernel(x, ln_even, w_in_even, conv_w, q_gain, k_gain, w_out_even, rel_bias, ln_odd, w_in_odd,
           lower_bounds, o_gain, w_out_odd):
    batch, seq, d = x.shape
    assert d == D_MODEL and seq % (DILATED_PATTERNS[-1][1] * ATTN_BLOCK) == 0 and seq % PROJ_TM == 0
    depth = ln_even.shape[0] + ln_odd.shape[0]
    x2d = x.reshape(batch * seq, d).astype(F32)

    lbs = jnp.cumsum(jax.nn.softmax(lower_bounds.astype(F32), axis=0), axis=0)
    lbs = lbs - lbs[0:1]
    biases = [_band_bias(rel_bias, window, dilation) for window, dilation in DILATED_PATTERNS]
    ones = jnp.ones((1, IN_WIDTH), F32)
    q_tile, k_tile = EVEN_PLAN[0][1], EVEN_PLAN[1][1]
    w_in_even, w_in_odd, w_out_even, w_out_odd = (
        w.astype(BF16) for w in (w_in_even, w_in_odd, w_out_even, w_out_odd))

    for layer in range(depth):
        j = layer // 2
        if layer % 2 == 0:
            q_scale = jnp.tile(q_gain[j].astype(F32), ATTN_HEADS) * (ATTN_HEAD_DIM ** -0.5 * LOG2E)
            colscale = ones.at[0, q_tile * PROJ_TN:(q_tile + 1) * PROJ_TN].set(q_scale)
            colscale = colscale.at[0, k_tile * PROJ_TN:(k_tile + 1) * PROJ_TN].set(
                jnp.tile(k_gain[j].astype(F32), ATTN_HEADS))
            proj, qkv4, qkv16 = _norm_inproj(x2d, ln_even[j], w_in_even, j, colscale, ones,
                                             EVEN_PLAN, False, batch, seq)
            prev = None
            for index, srcs in enumerate((proj, qkv4, qkv16)):
                prev = _attention_pattern(index, srcs, biases[index], prev, batch, seq)
            x2d = _outproj_even(proj, prev[0], conv_w[j], w_out_even, j, x2d, seq)
        else:
            gate_tiles = [jt for kind, jt, _ in ODD_PLAN if kind == "log2gate"]
            gate_cols = slice(min(gate_tiles) * PROJ_TN, (max(gate_tiles) + 1) * PROJ_TN)
            gate_mid = ones.at[0, gate_cols].set(0.5 * (1.0 + lbs[j]))
            gate_half_range = ones.at[0, gate_cols].set(0.5 * (1.0 - lbs[j]))
            (proj,) = _norm_inproj(x2d, ln_odd[j], w_in_odd, j, gate_mid, gate_half_range,
                                   ODD_PLAN, j == 0, batch, seq)
            y = _hgrn2(proj, o_gain[j], batch, seq)
            x2d = _outproj(y, w_out_odd, j, x2d)
    return x2d.reshape(batch, seq, d).astype(x.dtype)
```

```python
import functools
import math

import jax
import jax.numpy as jnp
import numpy as np
from jax import lax
from jax.experimental import pallas as pl
from jax.experimental.pallas import tpu as pltpu

F32 = jnp.float32
BF16 = jnp.bfloat16

D_MODEL = 1024
MIX_WIDTH = 2 * D_MODEL
CONV_WIDTH = MIX_WIDTH // 2
ATTN_HEAD_DIM = 64
ATTN_HEADS = 16
ATTN_WIDTH = ATTN_HEADS * ATTN_HEAD_DIM
ATTN_BLOCK = 128
DILATED_PATTERNS = ((128, 1), (512, 4), (2048, 16))
DILATION_STEP = 4
REL_BUCKETS = 32
REL_MAX_DISTANCE = 2048
HGRN_HEADS = 16
HGRN_DIM = 128
IN_WIDTH = 8192
EPS = 1e-6
MASKED = -1e30
LOG2E = math.log2(math.e)

V7X_LANES = 128
V7X_SUBLANES = 8
V7X_VMEM_LIMIT_BYTES = 56 * 1024 * 1024

PROJ_TM = 512
INPROJ_TM_EVEN = 256
INPROJ_TM_ODD = 512
PROJ_TN = 1024
ATTN_ROWS = 512
ATTN_UNITS_PER_GROUP = 4
HGRN_ROWS = 1024
HGRN_CHUNK = 128
HGRN_HEADS_PER_STEP = 4
HGRN_HEADS_PER_GROUP = 2
HGRN_MAX_HALF_DECAY = 100.0

N_SLABS = ATTN_WIDTH // V7X_LANES


def _cparams(semantics):
    return pltpu.CompilerParams(dimension_semantics=semantics, vmem_limit_bytes=V7X_VMEM_LIMIT_BYTES)


def _silu(t):
    half = 0.5 * t
    return half + half * jnp.tanh(half)


EVEN_PLAN = (("headnorm", 3, 0), ("headnorm", 4, 1), ("scatter", 5, 2),
             ("gate_product", (0, 6), 3),
             ("product", (1, 2), 4),
             ("silu", 7, 5))
ODD_PLAN = (("silu", 0, 0), ("silu", 1, 1), ("log2gate", 2, 2), ("log2gate", 3, 3),
            ("plain", 4, 4), ("plain", 5, 5), ("silu", 6, 6), ("silu", 7, 7))
QKV_OUT_TILES = 3
EVEN_OUT_WIDTH = len(EVEN_PLAN) * PROJ_TN
G1_TILE, U_TILE, Z2_TILE = 3, 4, 5


def _log2_forget_gate(f_pre, lb_terms):
    if lb_terms is None:
        log_f = jnp.minimum(f_pre, 0.0) - jnp.log(1.0 + jnp.exp(-jnp.abs(f_pre)))
    else:
        mid, half_range = lb_terms
        log_f = jnp.log(mid + half_range * jnp.tanh(0.5 * f_pre))
    return log_f * LOG2E


def _residue_major_permutation(rows, dilation):
    out = np.arange(rows)
    src = dilation * (out % (rows // dilation)) + out // (rows // dilation)
    return jnp.asarray(src[:, None] == np.arange(rows)[None, :], BF16)


def _inproj_body(x_ref, g_ref, w_ref, ca_ref, cb_ref, bd_ref, *rest, plan, zero_lb):
    scatter = any(kind == "scatter" for kind, _, _ in plan)
    if scatter:
        p4_ref, p16_ref, o_ref, o4_ref, o16_ref = rest
    else:
        (o_ref,) = rest
    x = x_ref[...]
    ms = jnp.mean(x * x, axis=-1, keepdims=True)
    hn = (x * lax.rsqrt(ms + EPS) * g_ref[...]).astype(BF16)
    tm = x.shape[0]
    tn = PROJ_TN

    def project(jt):
        return jnp.dot(hn, w_ref[:, jt * tn:(jt + 1) * tn], preferred_element_type=F32)

    def emit_residue_major(res, cols, part):
        res = res.astype(BF16)
        o_ref[:, cols] = res
        for perm_ref, ref in ((p4_ref, o4_ref), (p16_ref, o16_ref)):
            dil = ref.shape[0]
            moved = jnp.dot(perm_ref[...], res, preferred_element_type=F32).astype(ref.dtype)
            for r in range(dil):
                ref[r, :, part * tn:(part + 1) * tn] = moved[r * (tm // dil):(r + 1) * (tm // dil)]

    for kind, jt, out_tile in plan:
        cols = slice(out_tile * tn, (out_tile + 1) * tn)
        if kind == "product":
            o_ref[:, cols] = (project(jt[0]) * project(jt[1])).astype(o_ref.dtype)
        elif kind == "gate_product":
            o_ref[:, cols] = (project(jt[0]) * _silu(project(jt[1]))).astype(o_ref.dtype)
        elif kind == "plain":
            o_ref[:, cols] = project(jt).astype(o_ref.dtype)
        elif kind == "silu":
            o_ref[:, cols] = _silu(project(jt)).astype(o_ref.dtype)
        elif kind == "log2gate":
            wcols = slice(jt * tn, (jt + 1) * tn)
            lb_terms = None if zero_lb else (ca_ref[:, wcols], cb_ref[:, wcols])
            o_ref[:, cols] = _log2_forget_gate(project(jt), lb_terms).astype(o_ref.dtype)
        elif kind == "scatter":
            emit_residue_major(project(jt), cols, out_tile)
        else:
            acc = project(jt)
            width = bd_ref.shape[0]
            parts = []
            for s in range(tn // width):
                a = acc[:, s * width:(s + 1) * width]
                ms = jnp.dot((a * a).astype(BF16), bd_ref[...], preferred_element_type=F32)
                parts.append(a * lax.rsqrt(ms + EPS) * ca_ref[:, jt * tn + s * width:jt * tn + (s + 1) * width])
            emit_residue_major(jnp.concatenate(parts, axis=1), cols, out_tile)


def _norm_inproj(x2d, gain, w_bf16, layer, col_a, col_b, plan, zero_lb, batch, seq):
    t, d = x2d.shape
    n = w_bf16.shape[2]
    scatter = any(kind == "scatter" for kind, _, _ in plan)
    tm, tn = (INPROJ_TM_EVEN if scatter else INPROJ_TM_ODD), PROJ_TN
    n_out = len(plan) * tn
    width = 2 * V7X_LANES
    head_of_lane = np.arange(width) // ATTN_HEAD_DIM
    bd = jnp.asarray((head_of_lane[:, None] == head_of_lane[None, :]) / ATTN_HEAD_DIM, BF16)
    tiles_per_seq = seq // tm
    out_specs = [pl.BlockSpec((tm, n_out), lambda i: (i, 0))]
    out_shape = [jax.ShapeDtypeStruct((t, n_out), BF16)]
    perms = []
    if scatter:
        qkv_width = QKV_OUT_TILES * tn
        for _, dil in DILATED_PATTERNS[1:]:
            out_specs.append(pl.BlockSpec((None, dil, tm // dil, qkv_width),
                                          lambda i: (i // tiles_per_seq, 0, i % tiles_per_seq, 0)))
            out_shape.append(jax.ShapeDtypeStruct((batch, dil, seq // dil, qkv_width), BF16))
            perms.append(_residue_major_permutation(tm, dil))
    resident = dict(pipeline_mode=pl.Buffered(1))
    return pl.pallas_call(
        functools.partial(_inproj_body, plan=plan, zero_lb=zero_lb),
        grid=(t // tm,),
        in_specs=[
            pl.BlockSpec((tm, d), lambda i: (i, 0)),
            pl.BlockSpec((1, d), lambda i: (0, 0)),
            pl.BlockSpec((None, d, n), lambda i: (layer, 0, 0), **resident),
            pl.BlockSpec((1, n), lambda i: (0, 0)),
            pl.BlockSpec((1, n), lambda i: (0, 0)),
            pl.BlockSpec((width, width), lambda i: (0, 0)),
        ] + [pl.BlockSpec((tm, tm), lambda i: (0, 0))] * len(perms),
        out_specs=out_specs,
        out_shape=out_shape,
        compiler_params=_cparams(("parallel",)),
        name="norm_inproj_even" if scatter else "norm_inproj_odd",
    )(x2d, gain.reshape(1, d).astype(F32), w_bf16, col_a, col_b, bd, *perms)


def _attn_body(*refs, has_prev, scatter, rows):
    refs = list(refs)
    qkv_ref, pqkv_ref, bias_ref = refs[:3]
    del refs[:3]
    if has_prev:
        po_ref, pl_ref = refs[:2]
        del refs[:2]
    if scatter:
        o_ref, l_ref, so_ref, sl_ref = refs
    else:
        (o_ref,) = refs
    blk = ATTN_BLOCK
    width = ATTN_WIDTH
    t = pl.program_id(2)

    def window(part, qb, c0):
        cols = slice(part * width + c0, part * width + c0 + V7X_LANES)
        if qb == 0:
            return jnp.concatenate([pqkv_ref[:, cols], qkv_ref[0:blk, cols]], axis=0)
        return qkv_ref[(qb - 1) * blk:(qb + 1) * blk, cols]

    lane = lax.broadcasted_iota(jnp.int32, (blk, V7X_LANES), 1)
    low = lane < ATTN_HEAD_DIM
    low_bf = jnp.where(low, 1.0, 0.0).astype(BF16)
    high_bf = jnp.where(low, 0.0, 1.0).astype(BF16)
    low2_bf = jnp.concatenate([low_bf, low_bf], axis=0)
    high2_bf = jnp.concatenate([high_bf, high_bf], axis=0)
    n_pairs = ATTN_HEADS // 2
    step = DILATION_STEP
    sub = blk // step

    def locate(idx):
        qb = idx // n_pairs
        hp = idx % n_pairs
        return qb, hp, qb * blk, hp * V7X_LANES

    def logits(idx):
        qb, hp, r0, c0 = locate(idx)
        q2 = qkv_ref[r0:r0 + blk, c0:c0 + V7X_LANES]
        kw = window(1, qb, c0)
        qs = jnp.concatenate([q2 * low_bf, q2 * high_bf], axis=0)
        return lax.dot_general(qs, kw, (((1,), (1,)), ((), ())), preferred_element_type=F32)

    def softmax(idx, s):
        qb, hp, _, _ = locate(idx)
        first = (t == 0).astype(jnp.int32) if qb == 0 else 0
        s = s + jnp.concatenate([bias_ref[first, 2 * hp], bias_ref[first, 2 * hp + 1]], axis=0)
        m = jnp.max(s, axis=-1, keepdims=True)
        p = jnp.exp2(s - m).astype(BF16)
        return p, jnp.where(low, m[0:blk], m[blk:2 * blk])

    def weighted(idx, p):
        qb, _, _, c0 = locate(idx)
        vw = window(2, qb, c0)
        rhs = jnp.concatenate([jnp.concatenate([vw * low2_bf, low2_bf], axis=1),
                               jnp.concatenate([vw * high2_bf, high2_bf], axis=1)], axis=0)
        lhs = jnp.concatenate([p[0:blk], p[blk:2 * blk]], axis=1)
        return jnp.dot(lhs, rhs, preferred_element_type=F32)

    def finish(idx, slot, pv, m2):
        qb, _, r0, c0 = locate(idx)
        den = pv[:, V7X_LANES:]
        o2 = pv[:, 0:V7X_LANES] / den
        if has_prev or scatter:
            lse2 = m2 + jnp.log(den) * LOG2E
        if has_prev:
            po = po_ref[pl.ds(r0, blk), pl.ds(c0, V7X_LANES)].astype(F32)
            plse = pl_ref[pl.ds(r0, blk), pl.ds(c0, V7X_LANES)]
            gap = plse - lse2
            e = jnp.exp2(-jnp.abs(gap))
            prev_larger = gap >= 0.0
            tot = 1.0 + e
            o2 = (jnp.where(prev_larger, po, o2) + e * jnp.where(prev_larger, o2, po)) / tot
            lse2 = jnp.maximum(plse, lse2) + jnp.log(tot) * LOG2E
        if not scatter:
            o_ref[pl.ds(r0, blk), pl.ds(c0, V7X_LANES)] = o2.astype(o_ref.dtype)
            return
        so_ref[slot] = o2
        sl_ref[slot] = lse2
        d0 = qb * sub
        for jm in range(step):
            o_ref[jm, pl.ds(d0, sub), pl.ds(c0, V7X_LANES)] = so_ref[
                slot, pl.ds(jm, sub, stride=step), :].astype(o_ref.dtype)
            l_ref[jm, pl.ds(d0, sub), pl.ds(c0, V7X_LANES)] = sl_ref[slot, pl.ds(jm, sub, stride=step), :]

    per_group = ATTN_UNITS_PER_GROUP
    n_units = (rows // blk) * n_pairs
    groups = [list(range(g, g + per_group)) for g in range(0, n_units, per_group)]
    pending = None
    for units in groups + [None]:
        pvs = None
        if pending is not None:
            prev_units, probs = pending
            pvs = [weighted(u, p) for u, (p, _) in zip(prev_units, probs)]
        scores = [logits(u) for u in units] if units is not None else None
        if pvs is not None:
            for slot, (u, pv, (_, m2)) in enumerate(zip(prev_units, pvs, probs)):
                finish(u, slot, pv, m2)
        pending = (units, [softmax(u, s) for u, s in zip(units, scores)]) if units is not None else None


def _attention_pattern(index, srcs, bias, prev, batch, seq):
    dilation = DILATED_PATTERNS[index][1]
    last = index == len(DILATED_PATTERNS) - 1
    length = seq // dilation
    rows = min(ATTN_ROWS, length)
    blk = ATTN_BLOCK
    width = ATTN_WIDTH
    per_step = rows // blk

    qkv_width = QKV_OUT_TILES * width
    if index == 0:
        src = srcs.reshape(batch, seq, EVEN_OUT_WIDTH)
        cur = pl.BlockSpec((None, rows, qkv_width), lambda b, r, t: (b, t, 0))
        prv = pl.BlockSpec((None, blk, qkv_width), lambda b, r, t: (b, jnp.maximum(t * per_step - 1, 0), 0))
    else:
        src = srcs
        cur = pl.BlockSpec((None, None, rows, qkv_width), lambda b, r, t: (b, r, t, 0))
        prv = pl.BlockSpec((None, None, blk, qkv_width),
                           lambda b, r, t: (b, r, jnp.maximum(t * per_step - 1, 0), 0))

    res_blk = pl.BlockSpec((None, None, rows, width), lambda b, r, t: (b, r, t, 0))
    in_specs = [cur, prv, pl.BlockSpec(bias.shape, lambda b, r, t: (0, 0, 0, 0))]
    args = [src, src, bias]
    if prev is not None:
        in_specs += [res_blk, res_blk]
        args += [prev[0].reshape(batch, dilation, length, width), prev[1].reshape(batch, dilation, length, width)]
    scratch = []
    if last:
        out_specs = [res_blk]
        out_shape = [jax.ShapeDtypeStruct((batch, dilation, length, width), BF16)]
    else:
        step = DILATION_STEP
        shape = (batch, step, dilation, length // step, width)
        out_blk = pl.BlockSpec((None, step, None, rows // step, width), lambda b, r, t: (b, 0, r, t, 0))
        out_specs = [out_blk, out_blk]
        out_shape = [jax.ShapeDtypeStruct(shape, BF16), jax.ShapeDtypeStruct(shape, F32)]
        scratch += [pltpu.VMEM((ATTN_UNITS_PER_GROUP, blk, V7X_LANES), F32)] * 2
    return pl.pallas_call(
        functools.partial(_attn_body, has_prev=prev is not None, scatter=not last, rows=rows),
        grid=(batch, dilation, length // rows),
        in_specs=in_specs,
        out_specs=out_specs,
        out_shape=out_shape,
        scratch_shapes=scratch,
        compiler_params=_cparams(("parallel", "parallel", "arbitrary")),
        name=f"dilated_attention_d{dilation}",
    )(*args)


def _t5_bucket(distance):
    max_exact = REL_BUCKETS // 2
    scaled = jnp.log(jnp.maximum(distance, max_exact).astype(F32) / max_exact) / math.log(
        REL_MAX_DISTANCE / max_exact)
    large = jnp.minimum(max_exact + (scaled * (REL_BUCKETS - max_exact)).astype(jnp.int32), REL_BUCKETS - 1)
    return jnp.where(distance < max_exact, distance, large)


def _band_bias(rel_bias, window, dilation):
    blk = ATTN_BLOCK
    n_back = window // dilation
    qi = jnp.arange(blk)[:, None]
    kj = jnp.arange(2 * blk)[None, :]
    delta = blk + qi - kj
    band = (delta >= 0) & (delta <= n_back)
    onehot = jax.nn.one_hot(_t5_bucket(jnp.maximum(delta, 0) * dilation), REL_BUCKETS, dtype=F32)
    bias = jnp.einsum("qkb,bh->hqk", onehot, rel_bias.astype(F32) * LOG2E, precision=lax.Precision.HIGHEST)
    bias = jnp.where(band[None], bias, MASKED)
    return jnp.stack([bias, jnp.where(kj[None] >= blk, bias, MASKED)])


def _outproj_even_body(g1_ref, u_ref, z2_ref, oa_ref, uh_ref, cw_ref, w_ref, x_ref, o_ref, slab_ref, *,
                       tiles_per_seq):
    i = pl.program_id(0)
    tm = u_ref.shape[0]
    sub = V7X_SUBLANES
    half = CONV_WIDTH
    dil = oa_ref.shape[0]
    kc = 2 * V7X_LANES
    acc = x_ref[...]
    for c0 in range(0, half, kc):
        for s in range(c0 // V7X_LANES, (c0 + kc) // V7X_LANES):
            for r in range(dil):
                slab_ref[s, pl.ds(r, tm // dil, stride=dil), :] = oa_ref[
                    r, :, s * V7X_LANES:(s + 1) * V7X_LANES].astype(F32)
        attn = jnp.concatenate([slab_ref[s] for s in range(c0 // V7X_LANES, (c0 + kc) // V7X_LANES)], axis=1)
        yb = attn.astype(BF16) * z2_ref[:, c0:c0 + kc]
        acc += jnp.dot(yb, w_ref[half + c0:half + c0 + kc, :], preferred_element_type=F32)

    first = (i % tiles_per_seq) == 0
    row = lax.broadcasted_iota(jnp.int32, (sub, kc), 0)
    for c0 in range(0, half, kc):
        cols = slice(c0, c0 + kc)
        u = u_ref[:, cols].astype(F32)
        halo = jnp.where(first, 0.0, uh_ref[:, cols].astype(F32))
        r1 = pltpu.roll(u, 1, axis=0)
        r2 = pltpu.roll(u, 2, axis=0)
        u1 = jnp.concatenate([jnp.where(row >= 1, r1[0:sub], pltpu.roll(halo, 1, axis=0)), r1[sub:]], axis=0)
        u2 = jnp.concatenate([jnp.where(row >= 2, r2[0:sub], pltpu.roll(halo, 2, axis=0)), r2[sub:]], axis=0)
        conv = cw_ref[0:1, cols] * u2 + cw_ref[1:2, cols] * u1 + cw_ref[2:3, cols] * u
        ya = (g1_ref[:, cols].astype(F32) * conv).astype(BF16)
        acc += jnp.dot(ya, w_ref[cols, :], preferred_element_type=F32)
    o_ref[...] = acc


def _outproj_even(proj, o_attn, conv_w, w_bf16, layer, x2d, seq):
    t, d = x2d.shape
    tm = PROJ_TM
    w = CONV_WIDTH
    halo_blocks = tm // V7X_SUBLANES
    tiles_per_seq = seq // tm
    dil = o_attn.shape[1]

    def colblk(c):
        return pl.BlockSpec((tm, w), lambda i: (i, c))

    return pl.pallas_call(
        functools.partial(_outproj_even_body, tiles_per_seq=tiles_per_seq),
        grid=(t // tm,),
        in_specs=[colblk(G1_TILE), colblk(U_TILE), colblk(Z2_TILE),
                  pl.BlockSpec((None, dil, tm // dil, w), lambda i: (i // tiles_per_seq, 0, i % tiles_per_seq, 0)),
                  pl.BlockSpec((V7X_SUBLANES, w), lambda i: (jnp.maximum(i * halo_blocks - 1, 0), U_TILE)),
                  pl.BlockSpec(conv_w.shape, lambda i: (0, 0)),
                  pl.BlockSpec((None,) + w_bf16.shape[1:], lambda i: (layer, 0, 0)),
                  pl.BlockSpec((tm, d), lambda i: (i, 0))],
        out_specs=pl.BlockSpec((tm, d), lambda i: (i, 0)),
        out_shape=jax.ShapeDtypeStruct((t, d), F32),
        scratch_shapes=[pltpu.VMEM((N_SLABS, tm, V7X_LANES), F32)],
        compiler_params=_cparams(("parallel",)),
        name="conv_gate_outproj",
    )(proj, proj, proj, o_attn, proj, conv_w.astype(F32), w_bf16, x2d)


def _hgrn_body(q_ref, f_ref, i_ref, z_ref, og_ref, y_ref,
               st_ref, st0_ref, qrows_ref, frows_ref, krows_ref, *, rows):
    c = HGRN_CHUNK
    hd = HGRN_DIM
    heads = q_ref.shape[2] // hd
    chunks = rows // c
    t = pl.program_id(2)

    @pl.when(t == 0)
    def _():
        st_ref[...] = jnp.zeros_like(st_ref)

    st0_ref[...] = st_ref[...]
    ri = lax.broadcasted_iota(jnp.int32, (c, c), 0)
    ci = lax.broadcasted_iota(jnp.int32, (c, c), 1)
    causal = ri >= ci

    def lanes(h):
        return slice(h * hd, (h + 1) * hd)

    def finish(o, sl, h):
        ms = jnp.mean(o * o, axis=-1, keepdims=True)
        y = o * lax.rsqrt(ms + EPS) * og_ref[:, lanes(h)] * z_ref[0, sl, lanes(h)].astype(F32)
        y_ref[0, sl, lanes(h)] = y.astype(y_ref.dtype)

    worst = [jnp.zeros((1, hd), F32)]
    pre = {}
    tril = jnp.where(causal, 1.0, 0.0).astype(BF16)

    def stage_prefix(units):
        for h, n in units:
            sl = slice(n * c, (n + 1) * c)
            pre[h, n] = dict(gcum=jnp.dot(tril, f_ref[0, sl, lanes(h)], preferred_element_type=F32))

    def stage_scale(units):
        for h, n in units:
            sl = slice(n * c, (n + 1) * c)
            d = pre[h, n]
            gcum = d["gcum"]
            q = q_ref[0, sl, lanes(h)].astype(F32)
            k = 1.0 - jnp.exp2(f_ref[0, sl, lanes(h)].astype(F32))
            rho = 0.5 * gcum[c - 1:c]
            worst[0] = jnp.minimum(worst[0], jnp.where(rho == rho, rho, -jnp.inf))
            d.update(qt=(q * jnp.exp2(gcum - rho)).astype(BF16), kh=(k * jnp.exp2(rho - gcum)).astype(BF16),
                     e_rho=jnp.exp2(rho), v=i_ref[0, sl, lanes(h)])

    def stage_state(units):
        for u in units:
            pre[u]["dst"] = lax.dot_general(pre[u]["v"], pre[u]["kh"], (((0,), (0,)), ((), ())),
                                            preferred_element_type=F32) * pre[u]["e_rho"]
        for h in sorted({h for h, _ in units}):
            st = st_ref[h]
            for n in range(chunks):
                d = pre[h, n]
                d["rhs"] = jnp.concatenate([d["kh"], (st * d["e_rho"]).astype(BF16)], axis=0)
                st = st * (d["e_rho"] * d["e_rho"]) + d["dst"]
            st_ref[h] = st

    def stage_scores(units):
        for u in units:
            pre[u]["so"] = lax.dot_general(pre[u]["qt"], pre[u]["rhs"], (((1,), (1,)), ((), ())),
                                           preferred_element_type=F32)

    def stage_output(units):
        for u in units:
            scores = jnp.where(causal, pre[u]["so"][:, :c], 0.0).astype(BF16)
            pre[u]["o"] = pre[u]["so"][:, c:] + jnp.dot(scores, pre[u]["v"], preferred_element_type=F32)
        for h, n in units:
            finish(pre[h, n]["o"], slice(n * c, (n + 1) * c), h)

    gsize = HGRN_HEADS_PER_GROUP
    groups = [[(h, n) for h in range(g, g + gsize) for n in range(chunks)] for g in range(0, heads, gsize)]
    stages = (stage_prefix, stage_scale, stage_state, stage_scores, stage_output)
    for tick in range(len(stages) + len(groups) - 1):
        for gi, units in enumerate(groups):
            if 0 <= tick - gi < len(stages):
                stages[tick - gi](units)
    worst = worst[0]

    @pl.when(jnp.min(worst) < -HGRN_MAX_HALF_DECAY)
    def _():
        lane = lax.broadcasted_iota(jnp.int32, (hd, c), 1)
        for h in range(heads):
            for n in range(chunks):
                sl = slice(n * c, (n + 1) * c)
                decay = jnp.exp2(f_ref[0, sl, lanes(h)].astype(F32))
                qrows_ref[...] = q_ref[0, sl, lanes(h)].astype(F32)
                frows_ref[...] = decay
                krows_ref[...] = 1.0 - decay
                vt = jnp.transpose(i_ref[0, sl, lanes(h)].astype(F32))

                def step(s, carry):
                    st, ot = carry
                    vcol = jnp.sum(jnp.where(lane == s, vt, 0.0), axis=-1, keepdims=True)
                    st = st * frows_ref[pl.ds(s, 1), :] + vcol * krows_ref[pl.ds(s, 1), :]
                    ocol = jnp.sum(st * qrows_ref[pl.ds(s, 1), :], axis=-1, keepdims=True)
                    ot = jnp.where(lane == s, ocol, ot)
                    return st, ot

                st, ot = lax.fori_loop(0, c, step, (st0_ref[h], jnp.zeros((hd, c), F32)))
                st0_ref[h] = st
                finish(jnp.transpose(ot), sl, h)
        st_ref[...] = st0_ref[...]


def _hgrn2(proj, o_gain, batch, seq):
    rows = HGRN_ROWS
    hd = HGRN_DIM
    heads = HGRN_HEADS_PER_STEP
    width = heads * hd
    projv = proj.reshape(batch, seq, IN_WIDTH)
    per_kind = HGRN_HEADS // heads

    def part(kind):
        return pl.BlockSpec((1, rows, width), lambda b, h, t: (b, t, kind * per_kind + h))

    def per_head():
        return pl.BlockSpec((1, width), lambda b, h, t: (0, h))

    y = pl.pallas_call(
        functools.partial(_hgrn_body, rows=rows),
        grid=(batch, per_kind, seq // rows),
        in_specs=[part(0), part(1), part(2), part(3), per_head()],
        out_specs=pl.BlockSpec((1, rows, width), lambda b, h, t: (b, t, h)),
        out_shape=jax.ShapeDtypeStruct((batch, seq, MIX_WIDTH), BF16),
        scratch_shapes=[pltpu.VMEM((heads, hd, hd), F32), pltpu.VMEM((heads, hd, hd), F32)]
        + [pltpu.VMEM((HGRN_CHUNK, hd), F32)] * 3,
        compiler_params=_cparams(("parallel", "parallel", "arbitrary")),
        name="hgrn2_recurrence",
    )(projv, projv, projv, projv, o_gain.reshape(1, -1).astype(F32))
    return y.reshape(batch * seq, MIX_WIDTH)


def _outproj_body(y_ref, w_ref, x_ref, o_ref):
    o_ref[...] = x_ref[...] + jnp.dot(y_ref[...], w_ref[...], preferred_element_type=F32)


def _outproj(y, w_bf16, layer, x2d):
    t, d = x2d.shape
    tm = PROJ_TM
    return pl.pallas_call(
        _outproj_body,
        grid=(t // tm,),
        in_specs=[pl.BlockSpec((tm, y.shape[1]), lambda i: (i, 0)),
                  pl.BlockSpec((None,) + w_bf16.shape[1:], lambda i: (layer, 0, 0)),
                  pl.BlockSpec((tm, d), lambda i: (i, 0))],
        out_specs=pl.BlockSpec((tm, d), lambda i: (i, 0)),
        out_shape=jax.ShapeDtypeStruct((t, d), F32),
        compiler_params=_cparams(("parallel",)),
        name="outproj_residual",
    )(y, w_bf16, x2d)


def kernel(x, ln_even, w_in_even, conv_w, q_gain, k_gain, w_out_even, rel_bias, ln_odd, w_in_odd,
           lower_bounds, o_gain, w_out_odd):
    batch, seq, d = x.shape
    assert d == D_MODEL and seq % (DILATED_PATTERNS[-1][1] * ATTN_BLOCK) == 0 and seq % PROJ_TM == 0
    depth = ln_even.shape[0] + ln_odd.shape[0]
    x2d = x.reshape(batch * seq, d).astype(F32)

    lbs = jnp.cumsum(jax.nn.softmax(lower_bounds.astype(F32), axis=0), axis=0)
    lbs = lbs - lbs[0:1]
    biases = [_band_bias(rel_bias, window, dilation) for window, dilation in DILATED_PATTERNS]
    ones = jnp.ones((1, IN_WIDTH), F32)
    q_tile, k_tile = EVEN_PLAN[0][1], EVEN_PLAN[1][1]
    w_in_even, w_in_odd, w_out_even, w_out_odd = (
        w.astype(BF16) for w in (w_in_even, w_in_odd, w_out_even, w_out_odd))

    for layer in range(depth):
        j = layer // 2
        if layer % 2 == 0:
            q_scale = jnp.tile(q_gain[j].astype(F32), ATTN_HEADS) * (ATTN_HEAD_DIM ** -0.5 * LOG2E)
            colscale = ones.at[0, q_tile * PROJ_TN:(q_tile + 1) * PROJ_TN].set(q_scale)
            colscale = colscale.at[0, k_tile * PROJ_TN:(k_tile + 1) * PROJ_TN].set(
                jnp.tile(k_gain[j].astype(F32), ATTN_HEADS))
            proj, qkv4, qkv16 = _norm_inproj(x2d, ln_even[j], w_in_even, j, colscale, ones,
                                             EVEN_PLAN, False, batch, seq)
            prev = None
            for index, srcs in enumerate((proj, qkv4, qkv16)):
                prev = _attention_pattern(index, srcs, biases[index], prev, batch, seq)
            x2d = _outproj_even(proj, prev[0], conv_w[j], w_out_even, j, x2d, seq)
        else:
            gate_tiles = [jt for kind, jt, _ in ODD_PLAN if kind == "log2gate"]
            gate_cols = slice(min(gate_tiles) * PROJ_TN, (max(gate_tiles) + 1) * PROJ_TN)
            gate_mid = ones.at[0, gate_cols].set(0.5 * (1.0 + lbs[j]))
            gate_half_range = ones.at[0, gate_cols].set(0.5 * (1.0 - lbs[j]))
            (proj,) = _norm_inproj(x2d, ln_odd[j], w_in_odd, j, gate_mid, gate_half_range,
                                   ODD_PLAN, j == 0, batch, seq)
            y = _hgrn2(proj, o_gain[j], batch, seq)
            x2d = _outproj(y, w_out_odd, j, x2d)
    return x2d.reshape(batch, seq, d).astype(x.dtype)
```

```python
import functools
import math

import jax
import jax.numpy as jnp
import numpy as np
from jax import lax
from jax.experimental import pallas as pl
from jax.experimental.pallas import tpu as pltpu

F32 = jnp.float32
BF16 = jnp.bfloat16

D_MODEL = 1024
MIX_WIDTH = 2 * D_MODEL
CONV_WIDTH = MIX_WIDTH // 2
ATTN_HEAD_DIM = 64
ATTN_HEADS = 16
ATTN_WIDTH = ATTN_HEADS * ATTN_HEAD_DIM
ATTN_BLOCK = 128
DILATED_PATTERNS = ((128, 1), (512, 4), (2048, 16))
DILATION_STEP = 4
REL_BUCKETS = 32
REL_MAX_DISTANCE = 2048
HGRN_HEADS = 16
HGRN_DIM = 128
IN_WIDTH = 8192
EPS = 1e-6
MASKED = -1e30
LOG2E = math.log2(math.e)

V7X_LANES = 128
V7X_SUBLANES = 8
V7X_VMEM_LIMIT_BYTES = 56 * 1024 * 1024

PROJ_TM = 512
INPROJ_TM_EVEN = 256
INPROJ_TM_ODD = 512
PROJ_TN = 1024
ATTN_ROWS = 512
ATTN_UNITS_PER_GROUP = 4
HGRN_ROWS = 2048
HGRN_CHUNK = 128
HGRN_HEADS_PER_STEP = 4
HGRN_HEADS_PER_GROUP = 2
HGRN_MAX_HALF_DECAY = 100.0

N_SLABS = ATTN_WIDTH // V7X_LANES


def _cparams(semantics):
    return pltpu.CompilerParams(dimension_semantics=semantics, vmem_limit_bytes=V7X_VMEM_LIMIT_BYTES)


def _silu(t):
    half = 0.5 * t
    return half + half * jnp.tanh(half)


EVEN_PLAN = (("headnorm", 3, 0), ("headnorm", 4, 1), ("scatter", 5, 2),
             ("gate_product", (0, 6), 3),
             ("product", (1, 2), 4),
             ("silu", 7, 5))
ODD_PLAN = (("silu", 0, 0), ("silu", 1, 1), ("log2gate", 2, 2), ("log2gate", 3, 3),
            ("plain", 4, 4), ("plain", 5, 5), ("silu", 6, 6), ("silu", 7, 7))
QKV_OUT_TILES = 3
EVEN_OUT_WIDTH = len(EVEN_PLAN) * PROJ_TN
G1_TILE, U_TILE, Z2_TILE = 3, 4, 5


def _log2_forget_gate(f_pre, lb_terms):
    if lb_terms is None:
        log_f = jnp.minimum(f_pre, 0.0) - jnp.log(1.0 + jnp.exp(-jnp.abs(f_pre)))
    else:
        mid, half_range = lb_terms
        log_f = jnp.log(mid + half_range * jnp.tanh(0.5 * f_pre))
    return log_f * LOG2E


def _residue_major_permutation(rows, dilation):
    out = np.arange(rows)
    src = dilation * (out % (rows // dilation)) + out // (rows // dilation)
    return jnp.asarray(src[:, None] == np.arange(rows)[None, :], BF16)


def _inproj_body(x_ref, g_ref, w_ref, ca_ref, cb_ref, bd_ref, *rest, plan, zero_lb):
    scatter = any(kind == "scatter" for kind, _, _ in plan)
    if scatter:
        p4_ref, p16_ref, o_ref, o4_ref, o16_ref = rest
    else:
        (o_ref,) = rest
    x = x_ref[...]
    ms = jnp.mean(x * x, axis=-1, keepdims=True)
    hn = (x * lax.rsqrt(ms + EPS) * g_ref[...]).astype(BF16)
    tm = x.shape[0]
    tn = PROJ_TN

    def project(jt):
        return jnp.dot(hn, w_ref[:, jt * tn:(jt + 1) * tn], preferred_element_type=F32)

    def emit_residue_major(res, cols, part):
        res = res.astype(BF16)
        o_ref[:, cols] = res
        for perm_ref, ref in ((p4_ref, o4_ref), (p16_ref, o16_ref)):
            dil = ref.shape[0]
            moved = jnp.dot(perm_ref[...], res, preferred_element_type=F32).astype(ref.dtype)
            for r in range(dil):
                ref[r, :, part * tn:(part + 1) * tn] = moved[r * (tm // dil):(r + 1) * (tm // dil)]

    for kind, jt, out_tile in plan:
        cols = slice(out_tile * tn, (out_tile + 1) * tn)
        if kind == "product":
            o_ref[:, cols] = (project(jt[0]) * project(jt[1])).astype(o_ref.dtype)
        elif kind == "gate_product":
            o_ref[:, cols] = (project(jt[0]) * _silu(project(jt[1]))).astype(o_ref.dtype)
        elif kind == "plain":
            o_ref[:, cols] = project(jt).astype(o_ref.dtype)
        elif kind == "silu":
            o_ref[:, cols] = _silu(project(jt)).astype(o_ref.dtype)
        elif kind == "log2gate":
            wcols = slice(jt * tn, (jt + 1) * tn)
            lb_terms = None if zero_lb else (ca_ref[:, wcols], cb_ref[:, wcols])
            o_ref[:, cols] = _log2_forget_gate(project(jt), lb_terms).astype(o_ref.dtype)
        elif kind == "scatter":
            emit_residue_major(project(jt), cols, out_tile)
        else:
            acc = project(jt)
            width = bd_ref.shape[0]
            parts = []
            for s in range(tn // width):
                a = acc[:, s * width:(s + 1) * width]
                ms = jnp.dot((a * a).astype(BF16), bd_ref[...], preferred_element_type=F32)
                parts.append(a * lax.rsqrt(ms + EPS) * ca_ref[:, jt * tn + s * width:jt * tn + (s + 1) * width])
            emit_residue_major(jnp.concatenate(parts, axis=1), cols, out_tile)


def _norm_inproj(x2d, gain, w_bf16, layer, col_a, col_b, plan, zero_lb, batch, seq):
    t, d = x2d.shape
    n = w_bf16.shape[2]
    scatter = any(kind == "scatter" for kind, _, _ in plan)
    tm, tn = (INPROJ_TM_EVEN if scatter else INPROJ_TM_ODD), PROJ_TN
    n_out = len(plan) * tn
    width = 2 * V7X_LANES
    head_of_lane = np.arange(width) // ATTN_HEAD_DIM
    bd = jnp.asarray((head_of_lane[:, None] == head_of_lane[None, :]) / ATTN_HEAD_DIM, BF16)
    tiles_per_seq = seq // tm
    out_specs = [pl.BlockSpec((tm, n_out), lambda i: (i, 0))]
    out_shape = [jax.ShapeDtypeStruct((t, n_out), BF16)]
    perms = []
    if scatter:
        qkv_width = QKV_OUT_TILES * tn
        for _, dil in DILATED_PATTERNS[1:]:
            out_specs.append(pl.BlockSpec((None, dil, tm // dil, qkv_width),
                                          lambda i: (i // tiles_per_seq, 0, i % tiles_per_seq, 0)))
            out_shape.append(jax.ShapeDtypeStruct((batch, dil, seq // dil, qkv_width), BF16))
            perms.append(_residue_major_permutation(tm, dil))
    resident = dict(pipeline_mode=pl.Buffered(1))
    return pl.pallas_call(
        functools.partial(_inproj_body, plan=plan, zero_lb=zero_lb),
        grid=(t // tm,),
        in_specs=[
            pl.BlockSpec((tm, d), lambda i: (i, 0)),
            pl.BlockSpec((1, d), lambda i: (0, 0)),
            pl.BlockSpec((None, d, n), lambda i: (layer, 0, 0), **resident),
            pl.BlockSpec((1, n), lambda i: (0, 0)),
            pl.BlockSpec((1, n), lambda i: (0, 0)),
            pl.BlockSpec((width, width), lambda i: (0, 0)),
        ] + [pl.BlockSpec((tm, tm), lambda i: (0, 0))] * len(perms),
        out_specs=out_specs,
        out_shape=out_shape,
        compiler_params=_cparams(("parallel",)),
        name="norm_inproj_even" if scatter else "norm_inproj_odd",
    )(x2d, gain.reshape(1, d).astype(F32), w_bf16, col_a, col_b, bd, *perms)


def _attn_body(*refs, has_prev, scatter, rows):
    refs = list(refs)
    qkv_ref, pqkv_ref, bias_ref = refs[:3]
    del refs[:3]
    if has_prev:
        po_ref, pl_ref = refs[:2]
        del refs[:2]
    if scatter:
        o_ref, l_ref, so_ref, sl_ref = refs
    else:
        (o_ref,) = refs
    blk = ATTN_BLOCK
    width = ATTN_WIDTH
    t = pl.program_id(2)

    def window(part, qb, c0):
        cols = slice(part * width + c0, part * width + c0 + V7X_LANES)
        if qb == 0:
            return jnp.concatenate([pqkv_ref[:, cols], qkv_ref[0:blk, cols]], axis=0)
        return qkv_ref[(qb - 1) * blk:(qb + 1) * blk, cols]

    lane = lax.broadcasted_iota(jnp.int32, (blk, V7X_LANES), 1)
    low = lane < ATTN_HEAD_DIM
    low_bf = jnp.where(low, 1.0, 0.0).astype(BF16)
    high_bf = jnp.where(low, 0.0, 1.0).astype(BF16)
    low2_bf = jnp.concatenate([low_bf, low_bf], axis=0)
    high2_bf = jnp.concatenate([high_bf, high_bf], axis=0)
    n_pairs = ATTN_HEADS // 2
    step = DILATION_STEP
    sub = blk // step

    def locate(idx):
        qb = idx // n_pairs
        hp = idx % n_pairs
        return qb, hp, qb * blk, hp * V7X_LANES

    def logits(idx):
        qb, hp, r0, c0 = locate(idx)
        q2 = qkv_ref[r0:r0 + blk, c0:c0 + V7X_LANES]
        kw = window(1, qb, c0)
        qs = jnp.concatenate([q2 * low_bf, q2 * high_bf], axis=0)
        return lax.dot_general(qs, kw, (((1,), (1,)), ((), ())), preferred_element_type=F32)

    def softmax(idx, s):
        qb, hp, _, _ = locate(idx)
        first = (t == 0).astype(jnp.int32) if qb == 0 else 0
        s = s + jnp.concatenate([bias_ref[first, 2 * hp], bias_ref[first, 2 * hp + 1]], axis=0)
        m = jnp.max(s, axis=-1, keepdims=True)
        p = jnp.exp2(s - m).astype(BF16)
        return p, jnp.where(low, m[0:blk], m[blk:2 * blk])

    def weighted(idx, p):
        qb, _, _, c0 = locate(idx)
        vw = window(2, qb, c0)
        rhs = jnp.concatenate([jnp.concatenate([vw * low2_bf, low2_bf], axis=1),
                               jnp.concatenate([vw * high2_bf, high2_bf], axis=1)], axis=0)
        lhs = jnp.concatenate([p[0:blk], p[blk:2 * blk]], axis=1)
        return jnp.dot(lhs, rhs, preferred_element_type=F32)

    def finish(idx, slot, pv, m2):
        qb, _, r0, c0 = locate(idx)
        den = pv[:, V7X_LANES:]
        o2 = pv[:, 0:V7X_LANES] / den
        if has_prev or scatter:
            lse2 = m2 + jnp.log(den) * LOG2E
        if has_prev:
            po = po_ref[pl.ds(r0, blk), pl.ds(c0, V7X_LANES)].astype(F32)
            plse = pl_ref[pl.ds(r0, blk), pl.ds(c0, V7X_LANES)]
            gap = plse - lse2
            e = jnp.exp2(-jnp.abs(gap))
            prev_larger = gap >= 0.0
            tot = 1.0 + e
            o2 = (jnp.where(prev_larger, po, o2) + e * jnp.where(prev_larger, o2, po)) / tot
            lse2 = jnp.maximum(plse, lse2) + jnp.log(tot) * LOG2E
        if not scatter:
            o_ref[pl.ds(r0, blk), pl.ds(c0, V7X_LANES)] = o2.astype(o_ref.dtype)
            return
        so_ref[slot] = o2
        sl_ref[slot] = lse2
        d0 = qb * sub
        for jm in range(step):
            o_ref[jm, pl.ds(d0, sub), pl.ds(c0, V7X_LANES)] = so_ref[
                slot, pl.ds(jm, sub, stride=step), :].astype(o_ref.dtype)
            l_ref[jm, pl.ds(d0, sub), pl.ds(c0, V7X_LANES)] = sl_ref[slot, pl.ds(jm, sub, stride=step), :]

    per_group = ATTN_UNITS_PER_GROUP
    n_units = (rows // blk) * n_pairs
    groups = [list(range(g, g + per_group)) for g in range(0, n_units, per_group)]
    pending = None
    for units in groups + [None]:
        pvs = None
        if pending is not None:
            prev_units, probs = pending
            pvs = [weighted(u, p) for u, (p, _) in zip(prev_units, probs)]
        scores = [logits(u) for u in units] if units is not None else None
        if pvs is not None:
            for slot, (u, pv, (_, m2)) in enumerate(zip(prev_units, pvs, probs)):
                finish(u, slot, pv, m2)
        pending = (units, [softmax(u, s) for u, s in zip(units, scores)]) if units is not None else None


def _attention_pattern(index, srcs, bias, prev, batch, seq):
    dilation = DILATED_PATTERNS[index][1]
    last = index == len(DILATED_PATTERNS) - 1
    length = seq // dilation
    rows = min(ATTN_ROWS, length)
    blk = ATTN_BLOCK
    width = ATTN_WIDTH
    per_step = rows // blk

    qkv_width = QKV_OUT_TILES * width
    if index == 0:
        src = srcs.reshape(batch, seq, EVEN_OUT_WIDTH)
        cur = pl.BlockSpec((None, rows, qkv_width), lambda b, r, t: (b, t, 0))
        prv = pl.BlockSpec((None, blk, qkv_width), lambda b, r, t: (b, jnp.maximum(t * per_step - 1, 0), 0))
    else:
        src = srcs
        cur = pl.BlockSpec((None, None, rows, qkv_width), lambda b, r, t: (b, r, t, 0))
        prv = pl.BlockSpec((None, None, blk, qkv_width),
                           lambda b, r, t: (b, r, jnp.maximum(t * per_step - 1, 0), 0))

    res_blk = pl.BlockSpec((None, None, rows, width), lambda b, r, t: (b, r, t, 0))
    in_specs = [cur, prv, pl.BlockSpec(bias.shape, lambda b, r, t: (0, 0, 0, 0))]
    args = [src, src, bias]
    if prev is not None:
        in_specs += [res_blk, res_blk]
        args += [prev[0].reshape(batch, dilation, length, width), prev[1].reshape(batch, dilation, length, width)]
    scratch = []
    if last:
        out_specs = [res_blk]
        out_shape = [jax.ShapeDtypeStruct((batch, dilation, length, width), BF16)]
    else:
        step = DILATION_STEP
        shape = (batch, step, dilation, length // step, width)
        out_blk = pl.BlockSpec((None, step, None, rows // step, width), lambda b, r, t: (b, 0, r, t, 0))
        out_specs = [out_blk, out_blk]
        out_shape = [jax.ShapeDtypeStruct(shape, BF16), jax.ShapeDtypeStruct(shape, F32)]
        scratch += [pltpu.VMEM((ATTN_UNITS_PER_GROUP, blk, V7X_LANES), F32)] * 2
    return pl.pallas_call(
        functools.partial(_attn_body, has_prev=prev is not None, scatter=not last, rows=rows),
        grid=(batch, dilation, length // rows),
        in_specs=in_specs,
        out_specs=out_specs,
        out_shape=out_shape,
        scratch_shapes=scratch,
        compiler_params=_cparams(("parallel", "parallel", "arbitrary")),
        name=f"dilated_attention_d{dilation}",
    )(*args)


def _t5_bucket(distance):
    max_exact = REL_BUCKETS // 2
    scaled = jnp.log(jnp.maximum(distance, max_exact).astype(F32) / max_exact) / math.log(
        REL_MAX_DISTANCE / max_exact)
    large = jnp.minimum(max_exact + (scaled * (REL_BUCKETS - max_exact)).astype(jnp.int32), REL_BUCKETS - 1)
    return jnp.where(distance < max_exact, distance, large)


def _band_bias(rel_bias, window, dilation):
    blk = ATTN_BLOCK
    n_back = window // dilation
    qi = jnp.arange(blk)[:, None]
    kj = jnp.arange(2 * blk)[None, :]
    delta = blk + qi - kj
    band = (delta >= 0) & (delta <= n_back)
    onehot = jax.nn.one_hot(_t5_bucket(jnp.maximum(delta, 0) * dilation), REL_BUCKETS, dtype=F32)
    bias = jnp.einsum("qkb,bh->hqk", onehot, rel_bias.astype(F32) * LOG2E, precision=lax.Precision.HIGHEST)
    bias = jnp.where(band[None], bias, MASKED)
    return jnp.stack([bias, jnp.where(kj[None] >= blk, bias, MASKED)])


def _outproj_even_body(g1_ref, u_ref, z2_ref, oa_ref, uh_ref, cw_ref, w_ref, x_ref, o_ref, slab_ref, *,
                       tiles_per_seq):
    i = pl.program_id(0)
    tm = u_ref.shape[0]
    sub = V7X_SUBLANES
    half = CONV_WIDTH
    dil = oa_ref.shape[0]
    kc = 2 * V7X_LANES
    acc = x_ref[...]
    for c0 in range(0, half, kc):
        for s in range(c0 // V7X_LANES, (c0 + kc) // V7X_LANES):
            for r in range(dil):
                slab_ref[s, pl.ds(r, tm // dil, stride=dil), :] = oa_ref[
                    r, :, s * V7X_LANES:(s + 1) * V7X_LANES].astype(F32)
        attn = jnp.concatenate([slab_ref[s] for s in range(c0 // V7X_LANES, (c0 + kc) // V7X_LANES)], axis=1)
        yb = attn.astype(BF16) * z2_ref[:, c0:c0 + kc]
        acc += jnp.dot(yb, w_ref[half + c0:half + c0 + kc, :], preferred_element_type=F32)

    first = (i % tiles_per_seq) == 0
    row = lax.broadcasted_iota(jnp.int32, (sub, kc), 0)
    for c0 in range(0, half, kc):
        cols = slice(c0, c0 + kc)
        u = u_ref[:, cols].astype(F32)
        halo = jnp.where(first, 0.0, uh_ref[:, cols].astype(F32))
        r1 = pltpu.roll(u, 1, axis=0)
        r2 = pltpu.roll(u, 2, axis=0)
        u1 = jnp.concatenate([jnp.where(row >= 1, r1[0:sub], pltpu.roll(halo, 1, axis=0)), r1[sub:]], axis=0)
        u2 = jnp.concatenate([jnp.where(row >= 2, r2[0:sub], pltpu.roll(halo, 2, axis=0)), r2[sub:]], axis=0)
        conv = cw_ref[0:1, cols] * u2 + cw_ref[1:2, cols] * u1 + cw_ref[2:3, cols] * u
        ya = (g1_ref[:, cols].astype(F32) * conv).astype(BF16)
        acc += jnp.dot(ya, w_ref[cols, :], preferred_element_type=F32)
    o_ref[...] = acc


def _outproj_even(proj, o_attn, conv_w, w_bf16, layer, x2d, seq):
    t, d = x2d.shape
    tm = PROJ_TM
    w = CONV_WIDTH
    halo_blocks = tm // V7X_SUBLANES
    tiles_per_seq = seq // tm
    dil = o_attn.shape[1]

    def colblk(c):
        return pl.BlockSpec((tm, w), lambda i: (i, c))

    return pl.pallas_call(
        functools.partial(_outproj_even_body, tiles_per_seq=tiles_per_seq),
        grid=(t // tm,),
        in_specs=[colblk(G1_TILE), colblk(U_TILE), colblk(Z2_TILE),
                  pl.BlockSpec((None, dil, tm // dil, w), lambda i: (i // tiles_per_seq, 0, i % tiles_per_seq, 0)),
                  pl.BlockSpec((V7X_SUBLANES, w), lambda i: (jnp.maximum(i * halo_blocks - 1, 0), U_TILE)),
                  pl.BlockSpec(conv_w.shape, lambda i: (0, 0)),
                  pl.BlockSpec((None,) + w_bf16.shape[1:], lambda i: (layer, 0, 0)),
                  pl.BlockSpec((tm, d), lambda i: (i, 0))],
        out_specs=pl.BlockSpec((tm, d), lambda i: (i, 0)),
        out_shape=jax.ShapeDtypeStruct((t, d), F32),
        scratch_shapes=[pltpu.VMEM((N_SLABS, tm, V7X_LANES), F32)],
        compiler_params=_cparams(("parallel",)),
        name="conv_gate_outproj",
    )(proj, proj, proj, o_attn, proj, conv_w.astype(F32), w_bf16, x2d)


def _hgrn_body(q_ref, f_ref, i_ref, z_ref, og_ref, y_ref,
               st_ref, st0_ref, qrows_ref, frows_ref, krows_ref, *, rows):
    c = HGRN_CHUNK
    hd = HGRN_DIM
    heads = q_ref.shape[2] // hd
    chunks = rows // c
    t = pl.program_id(2)

    @pl.when(t == 0)
    def _():
        st_ref[...] = jnp.zeros_like(st_ref)

    st0_ref[...] = st_ref[...]
    ri = lax.broadcasted_iota(jnp.int32, (c, c), 0)
    ci = lax.broadcasted_iota(jnp.int32, (c, c), 1)
    causal = ri >= ci

    def lanes(h):
        return slice(h * hd, (h + 1) * hd)

    def finish(o, sl, h):
        ms = jnp.mean(o * o, axis=-1, keepdims=True)
        y = o * lax.rsqrt(ms + EPS) * og_ref[:, lanes(h)] * z_ref[0, sl, lanes(h)].astype(F32)
        y_ref[0, sl, lanes(h)] = y.astype(y_ref.dtype)

    worst = [jnp.zeros((1, hd), F32)]
    pre = {}
    tril = jnp.where(causal, 1.0, 0.0).astype(BF16)

    def stage_prefix(units):
        for h, n in units:
            sl = slice(n * c, (n + 1) * c)
            pre[h, n] = dict(gcum=jnp.dot(tril, f_ref[0, sl, lanes(h)], preferred_element_type=F32))

    def stage_scale(units):
        for h, n in units:
            sl = slice(n * c, (n + 1) * c)
            d = pre[h, n]
            gcum = d["gcum"]
            q = q_ref[0, sl, lanes(h)].astype(F32)
            k = 1.0 - jnp.exp2(f_ref[0, sl, lanes(h)].astype(F32))
            rho = 0.5 * gcum[c - 1:c]
            worst[0] = jnp.minimum(worst[0], jnp.where(rho == rho, rho, -jnp.inf))
            d.update(qt=(q * jnp.exp2(gcum - rho)).astype(BF16), kh=(k * jnp.exp2(rho - gcum)).astype(BF16),
                     e_rho=jnp.exp2(rho), v=i_ref[0, sl, lanes(h)])

    def stage_state(units):
        for u in units:
            pre[u]["dst"] = lax.dot_general(pre[u]["v"], pre[u]["kh"], (((0,), (0,)), ((), ())),
                                            preferred_element_type=F32) * pre[u]["e_rho"]
        for h in sorted({h for h, _ in units}):
            st = st_ref[h]
            for n in range(chunks):
                d = pre[h, n]
                d["rhs"] = jnp.concatenate([d["kh"], (st * d["e_rho"]).astype(BF16)], axis=0)
                st = st * (d["e_rho"] * d["e_rho"]) + d["dst"]
            st_ref[h] = st

    def stage_scores(units):
        for u in units:
            pre[u]["so"] = lax.dot_general(pre[u]["qt"], pre[u]["rhs"], (((1,), (1,)), ((), ())),
                                           preferred_element_type=F32)

    def stage_output(units):
        for u in units:
            scores = jnp.where(causal, pre[u]["so"][:, :c], 0.0).astype(BF16)
            pre[u]["o"] = pre[u]["so"][:, c:] + jnp.dot(scores, pre[u]["v"], preferred_element_type=F32)
        for h, n in units:
            finish(pre[h, n]["o"], slice(n * c, (n + 1) * c), h)

    gsize = HGRN_HEADS_PER_GROUP
    groups = [[(h, n) for h in range(g, g + gsize) for n in range(chunks)] for g in range(0, heads, gsize)]
    stages = (stage_prefix, stage_scale, stage_state, stage_scores, stage_output)
    for tick in range(len(stages) + len(groups) - 1):
        for gi, units in enumerate(groups):
            if 0 <= tick - gi < len(stages):
                stages[tick - gi](units)
    worst = worst[0]

    @pl.when(jnp.min(worst) < -HGRN_MAX_HALF_DECAY)
    def _():
        lane = lax.broadcasted_iota(jnp.int32, (hd, c), 1)
        for h in range(heads):
            for n in range(chunks):
                sl = slice(n * c, (n + 1) * c)
                decay = jnp.exp2(f_ref[0, sl, lanes(h)].astype(F32))
                qrows_ref[...] = q_ref[0, sl, lanes(h)].astype(F32)
                frows_ref[...] = decay
                krows_ref[...] = 1.0 - decay
                vt = jnp.transpose(i_ref[0, sl, lanes(h)].astype(F32))

                def step(s, carry):
                    st, ot = carry
                    vcol = jnp.sum(jnp.where(lane == s, vt, 0.0), axis=-1, keepdims=True)
                    st = st * frows_ref[pl.ds(s, 1), :] + vcol * krows_ref[pl.ds(s, 1), :]
                    ocol = jnp.sum(st * qrows_ref[pl.ds(s, 1), :], axis=-1, keepdims=True)
                    ot = jnp.where(lane == s, ocol, ot)
                    return st, ot

                st, ot = lax.fori_loop(0, c, step, (st0_ref[h], jnp.zeros((hd, c), F32)))
                st0_ref[h] = st
                finish(jnp.transpose(ot), sl, h)
        st_ref[...] = st0_ref[...]


def _hgrn2(proj, o_gain, batch, seq):
    rows = HGRN_ROWS
    hd = HGRN_DIM
    heads = HGRN_HEADS_PER_STEP
    width = heads * hd
    projv = proj.reshape(batch, seq, IN_WIDTH)
    per_kind = HGRN_HEADS // heads

    def part(kind):
        return pl.BlockSpec((1, rows, width), lambda b, h, t: (b, t, kind * per_kind + h))

    def per_head():
        return pl.BlockSpec((1, width), lambda b, h, t: (0, h))

    y = pl.pallas_call(
        functools.partial(_hgrn_body, rows=rows),
        grid=(batch, per_kind, seq // rows),
        in_specs=[part(0), part(1), part(2), part(3), per_head()],
        out_specs=pl.BlockSpec((1, rows, width), lambda b, h, t: (b, t, h)),
        out_shape=jax.ShapeDtypeStruct((batch, seq, MIX_WIDTH), BF16),
        scratch_shapes=[pltpu.VMEM((heads, hd, hd), F32), pltpu.VMEM((heads, hd, hd), F32)]
        + [pltpu.VMEM((HGRN_CHUNK, hd), F32)] * 3,
        compiler_params=_cparams(("parallel", "parallel", "arbitrary")),
        name="hgrn2_recurrence",
    )(projv, projv, projv, projv, o_gain.reshape(1, -1).astype(F32))
    return y.reshape(batch * seq, MIX_WIDTH)


def _outproj_body(y_ref, w_ref, x_ref, o_ref):
    o_ref[...] = x_ref[...] + jnp.dot(y_ref[...], w_ref[...], preferred_element_type=F32)


def _outproj(y, w_bf16, layer, x2d):
    t, d = x2d.shape
    tm = PROJ_TM
    return pl.pallas_call(
        _outproj_body,
        grid=(t // tm,),
        in_specs=[pl.BlockSpec((tm, y.shape[1]), lambda i: (i, 0)),
                  pl.BlockSpec((None,) + w_bf16.shape[1:], lambda i: (layer, 0, 0)),
                  pl.BlockSpec((tm, d), lambda i: (i, 0))],
        out_specs=pl.BlockSpec((tm, d), lambda i: (i, 0)),
        out_shape=jax.ShapeDtypeStruct((t, d), F32),
        compiler_params=_cparams(("parallel",)),
        name="outproj_residual",
    )(y, w_bf16, x2d)


def kernel(x, ln_even, w_in_even, conv_w, q_gain, k_gain, w_out_even, rel_bias, ln_odd, w_in_odd,
           lower_bounds, o_gain, w_out_odd):
    batch, seq, d = x.shape
    assert d == D_MODEL and seq % (DILATED_PATTERNS[-1][1] * ATTN_BLOCK) == 0 and seq % PROJ_TM == 0
    depth = ln_even.shape[0] + ln_odd.shape[0]
    x2d = x.reshape(batch * seq, d).astype(F32)

    lbs = jnp.cumsum(jax.nn.softmax(lower_bounds.astype(F32), axis=0), axis=0)
    lbs = lbs - lbs[0:1]
    biases = [_band_bias(rel_bias, window, dilation) for window, dilation in DILATED_PATTERNS]
    ones = jnp.ones((1, IN_WIDTH), F32)
    q_tile, k_tile = EVEN_PLAN[0][1], EVEN_PLAN[1][1]
    w_in_even, w_in_odd, w_out_even, w_out_odd = (
        w.astype(BF16) for w in (w_in_even, w_in_odd, w_out_even, w_out_odd))

    for layer in range(depth):
        j = layer // 2
        if layer % 2 == 0:
            q_scale = jnp.tile(q_gain[j].astype(F32), ATTN_HEADS) * (ATTN_HEAD_DIM ** -0.5 * LOG2E)
            colscale = ones.at[0, q_tile * PROJ_TN:(q_tile + 1) * PROJ_TN].set(q_scale)
            colscale = colscale.at[0, k_tile * PROJ_TN:(k_tile + 1) * PROJ_TN].set(
                jnp.tile(k_gain[j].astype(F32), ATTN_HEADS))
            proj, qkv4, qkv16 = _norm_inproj(x2d, ln_even[j], w_in_even, j, colscale, ones,
                                             EVEN_PLAN, False, batch, seq)
            prev = None
            for index, srcs in enumerate((proj, qkv4, qkv16)):
                prev = _attention_pattern(index, srcs, biases[index], prev, batch, seq)
            x2d = _outproj_even(proj, prev[0], conv_w[j], w_out_even, j, x2d, seq)
        else:
            gate_tiles = [jt for kind, jt, _ in ODD_PLAN if kind == "log2gate"]
            gate_cols = slice(min(gate_tiles) * PROJ_TN, (max(gate_tiles) + 1) * PROJ_TN)
            gate_mid = ones.at[0, gate_cols].set(0.5 * (1.0 + lbs[j]))
            gate_half_range = ones.at[0, gate_cols].set(0.5 * (1.0 - lbs[j]))
            (proj,) = _norm_inproj(x2d, ln_odd[j], w_in_odd, j, gate_mid, gate_half_range,
                                   ODD_PLAN, j == 0, batch, seq)
            y = _hgrn2(proj, o_gain[j], batch, seq)
            x2d = _outproj(y, w_out_odd, j, x2d)
    return x2d.reshape(batch, seq, d).astype(x.dtype)
```
